```python
import jax, jax.numpy as jnp
from jax import lax
import numpy as np

D_MODEL = 2048
BATCH = 4
SEQ = 8192
DEPTH = 2
DEC_BATCH = 16
DEC_SEQ = 64
PAST_LEN = 2048

CHUNK = 64
D_MIX = D_MODEL
MLA_HEADS = 8
MLA_NOPE = 128
MLA_ROPE = 64
MLA_V = 128
Q_LORA = 512
KV_LORA = 256
ROPE_THETA = 10000.0
Q_BLOCK = 128
ATTN_SCALE = (MLA_NOPE + MLA_ROPE) ** -0.5
CONV_DIM = 512
CONV_WIDTH = 3
SGU_HEADS = 4
SGU_HEAD_DIM = 128
SGU_DIM = SGU_HEADS * SGU_HEAD_DIM
SGU_CHUNK = 128
D_IN = Q_LORA + KV_LORA + MLA_ROPE + 3 * CONV_DIM + 2 * SGU_DIM
D_FF = 5632
N_EXPERTS = 8
TOP_K = 2
D_FF_EXPERT = 7168
N_DENSE = (DEPTH + 1) // 2
N_MOE = DEPTH // 2
ALPHA = (2 * DEPTH) ** 0.25
BETA = (8 * DEPTH) ** -0.25
LN_EPS = 1e-5
RMS_EPS = 1e-6

kernel_name = "hybrid_mla_conv_sgu_streaming_step"


def _layer_norm(x, g, b):
    xf = x.astype(jnp.float32)
    mu = jnp.mean(xf, -1, keepdims=True)
    var = jnp.mean(jnp.square(xf - mu), -1, keepdims=True)
    y = (xf - mu) * lax.rsqrt(var + LN_EPS) * g.astype(jnp.float32) + b.astype(jnp.float32)
    return y.astype(x.dtype)


def _rms_norm(x, g):
    xf = x.astype(jnp.float32)
    y = xf * lax.rsqrt(jnp.mean(jnp.square(xf), -1, keepdims=True) + RMS_EPS) * g.astype(jnp.float32)
    return y.astype(x.dtype)


def _rope(x, pos):
    half = MLA_ROPE // 2
    inv = ROPE_THETA ** (-jnp.arange(half, dtype=jnp.float32) / half)
    ang = pos.astype(jnp.float32)[:, None] * inv[None, :]
    shape = (ang.shape[0],) + (1,) * (x.ndim - 3) + (half,)
    cos = jnp.cos(ang).reshape(shape)
    sin = jnp.sin(ang).reshape(shape)
    xf = x.astype(jnp.float32)
    x1, x2 = xf[..., :half], xf[..., half:]
    return jnp.concatenate([x1 * cos - x2 * sin, x1 * sin + x2 * cos], -1).astype(x.dtype)


def _split_in(proj):
    sizes = (Q_LORA, KV_LORA, MLA_ROPE, CONV_DIM, CONV_DIM, CONV_DIM, SGU_DIM, SGU_DIM)
    idx, off = [], 0
    for s in sizes[:-1]:
        off += s
        idx.append(off)
    return jnp.split(proj, idx, axis=-1)


def _scores(q_nope, q_pe, k_nope, k_pe):
    s = jnp.einsum("bqhd,bkhd->bhqk", q_nope, k_nope, preferred_element_type=jnp.float32)
    s = s + jnp.einsum("bqhr,bkr->bhqk", q_pe, k_pe, preferred_element_type=jnp.float32)
    return s * ATTN_SCALE


def _mla_prompt(q_nope, q_pe, k_nope, k_pe, v):
    B, S = q_nope.shape[:2]
    nb = S // Q_BLOCK
    qn = q_nope.reshape(B, nb, Q_BLOCK, MLA_HEADS, MLA_NOPE).swapaxes(0, 1)
    qp = q_pe.reshape(B, nb, Q_BLOCK, MLA_HEADS, MLA_ROPE).swapaxes(0, 1)
    key_chunk = jnp.arange(S) // CHUNK

    def block(args):
        i, qn_b, qp_b = args
        s = _scores(qn_b, qp_b, k_nope, k_pe)
        q_chunk = (i * Q_BLOCK + jnp.arange(Q_BLOCK)) // CHUNK
        s = jnp.where(key_chunk[None, :] <= q_chunk[:, None], s, -jnp.inf)
        p = jax.nn.softmax(s, axis=-1).astype(v.dtype)
        return jnp.einsum("bhqk,bkhd->bqhd", p, v, preferred_element_type=jnp.float32).astype(v.dtype)

    o = lax.map(block, (jnp.arange(nb), qn, qp))
    return o.swapaxes(0, 1).reshape(B, S, MLA_HEADS * MLA_V)


def _mla_sample(q_nope, q_pe, k_nope, k_pe, v):
    B, T = q_nope.shape[:2]
    p = jax.nn.softmax(_scores(q_nope, q_pe, k_nope, k_pe), axis=-1).astype(v.dtype)
    o = jnp.einsum("bhqk,bkhd->bqhd", p, v, preferred_element_type=jnp.float32).astype(v.dtype)
    return o.reshape(B, T, MLA_HEADS * MLA_V)


def _short_conv(b_g, c_g, h_c, w, past):
    z = c_g * h_c
    zp = jnp.concatenate([past.astype(z.dtype), z], axis=1)
    y = lax.conv_general_dilated(zp, w[:, None, :].astype(z.dtype), (1,), "VALID",
                                 dimension_numbers=("NWC", "WIO", "NWC"),
                                 feature_group_count=CONV_DIM)
    return b_g * y, zp[:, -(CONV_WIDTH - 1):]


def _spatial_gate(u, v, w_s, b_s):
    B, T, _ = v.shape
    n = -(-T // SGU_CHUNK)
    pad = n * SGU_CHUNK - T
    vp = jnp.pad(v, ((0, 0), (0, pad), (0, 0))).reshape(B, n, SGU_CHUNK, SGU_HEADS, SGU_HEAD_DIM)
    w_m = w_s * jnp.tril(jnp.ones((SGU_CHUNK, SGU_CHUNK), w_s.dtype))
    mix = jnp.einsum("hpq,bnqhc->bnphc", w_m, vp) + b_s.T[None, None, :, :, None]
    mix = mix.reshape(B, n * SGU_CHUNK, SGU_DIM)[:, :T]
    return u * mix


def _swiglu(h, w1, w3, w2):
    return (jax.nn.silu(h @ w1) * (h @ w3)) @ w2


def _moe(h, w_r, b_r, w1, w3, w2):
    B, T, D = h.shape
    t = h.reshape(B * T, D)
    logits = jnp.dot(t, w_r, preferred_element_type=jnp.float32) + b_r.astype(jnp.float32)
    top_v, top_i = lax.top_k(logits, TOP_K)
    gates = jax.nn.softmax(top_v, axis=-1)
    gate = jnp.sum(jax.nn.one_hot(top_i, N_EXPERTS, dtype=jnp.float32) * gates[..., None], axis=1).astype(t.dtype)
    out = jnp.zeros_like(t)
    for e in range(N_EXPERTS):
        out = out + gate[:, e:e + 1] * _swiglu(t, w1[e], w3[e], w2[e])
    return out.reshape(B, T, D)


def _token_mixer(h, pos, l, p, ckv_past, kpe_past, conv_past):
    B, T, _ = h.shape
    q_lat, kv_lat, kpe_raw, b_g, c_g, h_c, u, v = _split_in(h @ p["w_in"][l])
    q = (_rms_norm(q_lat, p["q_norm_g"][l]) @ p["w_uq"][l]).reshape(B, T, MLA_HEADS, MLA_NOPE + MLA_ROPE)
    q_nope, q_pe = q[..., :MLA_NOPE], _rope(q[..., MLA_NOPE:], pos)
    ckv = _rms_norm(kv_lat, p["kv_norm_g"][l])
    kpe = _rope(kpe_raw, pos)
    if ckv_past is None:
        ckv_all, kpe_all = ckv, kpe
    else:
        ckv_all = jnp.concatenate([ckv_past.astype(ckv.dtype), ckv], axis=1)
        kpe_all = jnp.concatenate([kpe_past.astype(kpe.dtype), kpe], axis=1)
    kv = (ckv_all @ p["w_ukv"][l]).reshape(B, -1, MLA_HEADS, MLA_NOPE + MLA_V)
    k_nope, vv = kv[..., :MLA_NOPE], kv[..., MLA_NOPE:]
    if ckv_past is None:
        attn = _mla_prompt(q_nope, q_pe, k_nope, kpe_all, vv)
    else:
        attn = _mla_sample(q_nope, q_pe, k_nope, kpe_all, vv)
    if conv_past is None:
        conv_past = jnp.zeros((B, CONV_WIDTH - 1, CONV_DIM), h.dtype)
    conv_out, conv_state = _short_conv(b_g, c_g, h_c, p["conv_w"][l], conv_past)
    u = jax.nn.gelu(u, approximate=False)
    v = _layer_norm(jax.nn.gelu(v, approximate=False), p["sgu_ln_g"][l], p["sgu_ln_b"][l])
    sgu_out = _spatial_gate(u, v, p["sgu_w"][l], p["sgu_b"][l])
    o = jnp.concatenate([attn, conv_out, sgu_out], axis=-1) @ p["w_o"][l]
    return o, ckv, kpe, conv_state, v


def _layer(x, c, pos, l, p, ckv_past, kpe_past, conv_past):
    mod = (jax.nn.silu(c) @ p["w_ada"][l] + p["b_ada"][l])[:, None, :]
    sh1, sc1, g1, sh2, sc2, g2 = jnp.split(mod, 6, axis=-1)
    h = x * (1 + sc1) + sh1
    o, ckv, kpe, conv_state, v_rows = _token_mixer(h, pos, l, p, ckv_past, kpe_past, conv_past)
    x = _layer_norm(ALPHA * x + (1 + g1) * o, p["ln_g"][l, 0], p["ln_b"][l, 0])
    h = x * (1 + sc2) + sh2
    if l % 2 == 0:
        i = l // 2
        f = _swiglu(h, p["ffn_w1"][i], p["ffn_w3"][i], p["ffn_w2"][i])
    else:
        i = l // 2
        f = _moe(h, p["router_w"][i], p["router_b"][i], p["exp_w1"][i], p["exp_w3"][i], p["exp_w2"][i])
    x = _layer_norm(ALPHA * x + (1 + g2) * f, p["ln_g"][l, 1], p["ln_b"][l, 1])
    return x, ckv, kpe, conv_state, v_rows


def setup_inputs(seed: int = 0) -> dict:
    key = jax.random.key(seed)
    ks = iter(jax.random.split(key, 40))
    f32 = jnp.float32

    def nrm(shape, scale):
        return jax.random.normal(next(ks), shape, f32) * scale

    D = D_MODEL
    return {
        "x_prompt": nrm((BATCH, SEQ, D), 1.0),
        "x_sample": nrm((DEC_BATCH, DEC_SEQ, D), 1.0),
        "cache_ckv": nrm((DEPTH, DEC_BATCH, PAST_LEN, KV_LORA), 1.0),
        "cache_kpe": nrm((DEPTH, DEC_BATCH, PAST_LEN, MLA_ROPE), 1.0),
        "state_conv": nrm((DEPTH, DEC_BATCH, CONV_WIDTH - 1, CONV_DIM), 1.0),
        "c_prompt": nrm((BATCH, D), 1.0),
        "c_sample": nrm((DEC_BATCH, D), 1.0),
        "w_ada": nrm((DEPTH, D, 6 * D), 0.2 * D ** -0.5),
        "b_ada": nrm((DEPTH, 6 * D), 0.01),
        "w_in": nrm((DEPTH, D, D_IN), D ** -0.5),
        "q_norm_g": 1.0 + nrm((DEPTH, Q_LORA), 0.02),
        "kv_norm_g": 1.0 + nrm((DEPTH, KV_LORA), 0.02),
        "w_uq": nrm((DEPTH, Q_LORA, MLA_HEADS * (MLA_NOPE + MLA_ROPE)), Q_LORA ** -0.5),
        "w_ukv": nrm((DEPTH, KV_LORA, MLA_HEADS * (MLA_NOPE + MLA_V)), KV_LORA ** -0.5),
        "conv_w": nrm((DEPTH, CONV_WIDTH, CONV_DIM), CONV_WIDTH ** -0.5),
        "sgu_ln_g": 1.0 + nrm((DEPTH, SGU_DIM), 0.02),
        "sgu_ln_b": nrm((DEPTH, SGU_DIM), 0.02),
        "sgu_w": nrm((DEPTH, SGU_HEADS, SGU_CHUNK, SGU_CHUNK), SGU_CHUNK ** -0.5),
        "sgu_b": 1.0 + nrm((DEPTH, SGU_HEADS, SGU_CHUNK), 0.01),
        "w_o": nrm((DEPTH, D_MIX, D), BETA * D_MIX ** -0.5),
        "ln_g": 1.0 + nrm((DEPTH, 2, D), 0.02),
        "ln_b": nrm((DEPTH, 2, D), 0.02),
        "ffn_w1": nrm((N_DENSE, D, D_FF), D ** -0.5),
        "ffn_w3": nrm((N_DENSE, D, D_FF), D ** -0.5),
        "ffn_w2": nrm((N_DENSE, D_FF, D), BETA * D_FF ** -0.5),
        "router_w": nrm((N_MOE, D, N_EXPERTS), D ** -0.5),
        "router_b": nrm((N_MOE, N_EXPERTS), 0.01),
        "exp_w1": nrm((N_MOE, N_EXPERTS, D, D_FF_EXPERT), D ** -0.5),
        "exp_w3": nrm((N_MOE, N_EXPERTS, D, D_FF_EXPERT), D ** -0.5),
        "exp_w2": nrm((N_MOE, N_EXPERTS, D_FF_EXPERT, D), BETA * D_FF_EXPERT ** -0.5),
    }


def reference(x_prompt, x_sample, cache_ckv, cache_kpe, state_conv, c_prompt, c_sample,
              w_ada, b_ada, w_in, q_norm_g, kv_norm_g, w_uq, w_ukv, conv_w, sgu_ln_g, sgu_ln_b,
              sgu_w, sgu_b, w_o, ln_g, ln_b, ffn_w1, ffn_w3, ffn_w2, router_w, router_b,
              exp_w1, exp_w3, exp_w2):
    p = dict(w_ada=w_ada, b_ada=b_ada, w_in=w_in, q_norm_g=q_norm_g, kv_norm_g=kv_norm_g,
             w_uq=w_uq, w_ukv=w_ukv, conv_w=conv_w, sgu_ln_g=sgu_ln_g, sgu_ln_b=sgu_ln_b,
             sgu_w=sgu_w, sgu_b=sgu_b, w_o=w_o, ln_g=ln_g, ln_b=ln_b,
             ffn_w1=ffn_w1, ffn_w3=ffn_w3, ffn_w2=ffn_w2, router_w=router_w, router_b=router_b,
             exp_w1=exp_w1, exp_w3=exp_w3, exp_w2=exp_w2)
    pos_p = jnp.arange(x_prompt.shape[1])
    pos_s = cache_ckv.shape[2] + jnp.arange(x_sample.shape[1])
    yp, ys = x_prompt, x_sample
    ckv_p, kpe_p, conv_p = [], [], []
    ckv_s, kpe_s, conv_s, v_s = [], [], [], []
    for l in range(DEPTH):
        yp, ckv, kpe, cst, _ = _layer(yp, c_prompt, pos_p, l, p, None, None, None)
        ckv_p.append(ckv[:, -PAST_LEN:])
        kpe_p.append(kpe[:, -PAST_LEN:])
        conv_p.append(cst)
        ys, ckv, kpe, cst, v_rows = _layer(ys, c_sample, pos_s, l, p, cache_ckv[l], cache_kpe[l], state_conv[l])
        ckv_s.append(ckv)
        kpe_s.append(kpe)
        conv_s.append(cst)
        v_s.append(v_rows)
    return (yp, ys, jnp.stack(ckv_p), jnp.stack(kpe_p), jnp.stack(conv_p),
            jnp.stack(ckv_s), jnp.stack(kpe_s), jnp.stack(conv_s), jnp.stack(v_s))
```

```python
import functools

import numpy as np
import jax
import jax.numpy as jnp
from jax import lax
from jax.experimental import pallas as pl
from jax.experimental.pallas import tpu as pltpu

F32 = jnp.float32
BF16 = jnp.bfloat16
U32 = jnp.uint32

CHUNK = 64
MLA_HEADS = 8
MLA_NOPE = 128
MLA_ROPE = 64
MLA_V = 128
ROPE_THETA = 10000.0
ATTN_SCALE = (MLA_NOPE + MLA_ROPE) ** -0.5
CONV_WIDTH = 3
SGU_HEADS = 4
SGU_CHUNK = 128
TOP_K = 2
LN_EPS = 1e-5
RMS_EPS = 1e-6

LANE = 128
VMEM_LIMIT = 56 * 2 ** 20

HEAD_K = 2 * LANE
ROW_TILE = 512
FF_TILE = 512
ATTN_TILE = 512
COMBINE_TILE = 256


def _cparams(*sem):
    return pltpu.CompilerParams(dimension_semantics=sem, vmem_limit_bytes=VMEM_LIMIT)


def _const_spec(shape):
    nd = len(shape)
    return pl.BlockSpec(shape, lambda *_: (0,) * nd, pipeline_mode=pl.Buffered(1))


def _rms(x, g):
    return x * lax.rsqrt(jnp.mean(x * x, axis=-1, keepdims=True) + RMS_EPS) * g


def _ln(x, g, b):
    mu = jnp.mean(x, axis=-1, keepdims=True)
    xc = x - mu
    var = jnp.mean(xc * xc, axis=-1, keepdims=True)
    return xc * lax.rsqrt(var + LN_EPS) * g + b


def _gelu(x):
    return 0.5 * x * (1.0 + lax.erf(x * (2.0 ** -0.5)))


def _silu(x):
    return x * jax.nn.sigmoid(x)


def _per_chunk(x, fn):
    t, w = x.shape
    return fn(x.reshape(t // SGU_CHUNK, SGU_CHUNK, w)).reshape(t, w)


def _mod_kernel(c_ref, w_ref, b_ref, o_ref):
    a = _silu(c_ref[...]).astype(BF16)
    o_ref[...] = jnp.dot(a, w_ref[...].astype(BF16), preferred_element_type=F32) + b_ref[...]


def _adaln_mod(c_all, w_ada, b_ada):
    depth, d, m = w_ada.shape
    r = c_all.shape[0]
    tn = m // 8
    return pl.pallas_call(
        _mod_kernel,
        grid=(depth, m // tn),
        in_specs=[pl.BlockSpec((r, d), lambda l, j: (0, 0)),
                  pl.BlockSpec((None, d, tn), lambda l, j: (l, 0, j)),
                  pl.BlockSpec((None, 1, tn), lambda l, j: (l, 0, j))],
        out_specs=pl.BlockSpec((None, r, tn), lambda l, j: (l, 0, j)),
        out_shape=jax.ShapeDtypeStruct((depth, r, m), F32),
        compiler_params=_cparams("arbitrary", "arbitrary"),
        name="adaln_mod",
    )(c_all, w_ada, b_ada.reshape(depth, 1, m))


def _in_kernel(x_ref, sh_ref, sc_ref, cs_ref, w_in_ref, qg_ref, kvg_ref, wq_ref, wqr_ref,
               lng_ref, lnb_ref, q_out, ckv_out, kpe_out, z_out, bg_out, u_out, v_out,
               *, q_lora, kv_lora, conv_dim, sgu_dim):
    t, d = x_ref.shape
    nc = t // SGU_CHUNK
    x3 = x_ref[...].reshape(nc, SGU_CHUNK, d)
    h = (x3 * (1.0 + sc_ref[...]) + sh_ref[...]).reshape(t, d)
    proj = jnp.dot(h.astype(BF16), w_in_ref[...], preferred_element_type=F32)

    o = 0
    q_lat = proj[:, o:o + q_lora]; o += q_lora
    kv_lat = proj[:, o:o + kv_lora]; o += kv_lora
    kpe_raw = proj[:, o:o + LANE]; o += LANE
    kpe_rot = proj[:, o:o + LANE]; o += LANE
    b_g = proj[:, o:o + conv_dim]; o += conv_dim
    c_g = proj[:, o:o + conv_dim]; o += conv_dim
    h_c = proj[:, o:o + conv_dim]; o += conv_dim
    u = proj[:, o:o + sgu_dim]; o += sgu_dim
    v = proj[:, o:o + sgu_dim]

    cos = cs_ref[:, :LANE]
    sin = cs_ref[:, LANE:]
    ckv_out[...] = _rms(kv_lat, kvg_ref[...])
    kpe_out[...] = kpe_raw * cos + kpe_rot * sin

    qn = _rms(q_lat, qg_ref[...]).astype(BF16)
    raw = jnp.dot(qn, wq_ref[...], preferred_element_type=F32)
    rot = jnp.dot(qn, wqr_ref[...], preferred_element_type=F32)
    for hd in range(MLA_HEADS):
        a = hd * HEAD_K
        q_out[:, a:a + LANE] = (raw[:, a:a + LANE] * ATTN_SCALE).astype(BF16)
        pe = raw[:, a + LANE:a + HEAD_K] * cos + rot[:, hd * LANE:(hd + 1) * LANE] * sin
        q_out[:, a + LANE:a + HEAD_K] = (pe * ATTN_SCALE).astype(BF16)

    bg_out[...] = b_g
    z_out[...] = c_g * h_c
    u_out[...] = _gelu(u)
    v_out[...] = _ln(_gelu(v), lng_ref[...], lnb_ref[...])


def _in_proj(x, mod3, cs, w_in_p, qg, kvg, wq_p, wq_rot, lng, lnb):
    n, d = x.shape
    q_lora, kv_lora = qg.shape[1], kvg.shape[1]
    sgu_dim = lng.shape[1]
    conv_dim = (w_in_p.shape[1] - q_lora - kv_lora - 2 * LANE - 2 * sgu_dim) // 3
    t = ROW_TILE
    nc = t // SGU_CHUNK
    row = lambda w: pl.BlockSpec((t, w), lambda i: (i, 0))
    mod = lambda k: pl.BlockSpec((nc, 1, d), lambda i: (i, 0, k))
    kern = functools.partial(_in_kernel, q_lora=q_lora, kv_lora=kv_lora, conv_dim=conv_dim,
                             sgu_dim=sgu_dim)
    widths = (MLA_HEADS * HEAD_K, kv_lora, LANE, conv_dim, conv_dim, sgu_dim, sgu_dim)
    dtypes = (BF16, F32, F32, F32, F32, F32, F32)
    return pl.pallas_call(
        kern,
        grid=(n // t,),
        in_specs=[row(d), mod(0), mod(1), row(2 * LANE), _const_spec(w_in_p.shape),
                  _const_spec(qg.shape), _const_spec(kvg.shape), _const_spec(wq_p.shape),
                  _const_spec(wq_rot.shape), _const_spec(lng.shape), _const_spec(lnb.shape)],
        out_specs=[row(w) for w in widths],
        out_shape=[jax.ShapeDtypeStruct((n, w), dt) for w, dt in zip(widths, dtypes)],
        compiler_params=_cparams("arbitrary"),
        name="in_proj",
    )(x, mod3, mod3, cs, w_in_p, qg, kvg, wq_p, wq_rot, lng, lnb)


def _kv_kernel(ckv_ref, kpe_ref, wk_ref, wv_ref, k_out, v_out):
    c = ckv_ref[...].astype(BF16)
    kn = jnp.dot(c, wk_ref[...], preferred_element_type=F32)
    v_out[...] = jnp.dot(c, wv_ref[...], preferred_element_type=F32).astype(BF16)
    kpe = kpe_ref[...].astype(BF16)
    for hd in range(MLA_HEADS):
        a = hd * HEAD_K
        k_out[:, a:a + LANE] = kn[:, hd * MLA_NOPE:(hd + 1) * MLA_NOPE].astype(BF16)
        k_out[:, a + LANE:a + HEAD_K] = kpe


def _kv_expand(ckv, kpe, wk, wv):
    n, kv_lora = ckv.shape
    t = ROW_TILE
    row = lambda w: pl.BlockSpec((t, w), lambda i: (i, 0))
    return pl.pallas_call(
        _kv_kernel,
        grid=(n // t,),
        in_specs=[row(kv_lora), row(LANE), _const_spec(wk.shape), _const_spec(wv.shape)],
        out_specs=[row(MLA_HEADS * HEAD_K), row(MLA_HEADS * MLA_V)],
        out_shape=[jax.ShapeDtypeStruct((n, MLA_HEADS * HEAD_K), BF16),
                   jax.ShapeDtypeStruct((n, MLA_HEADS * MLA_V), BF16)],
        compiler_params=_cparams("arbitrary"),
        name="kv_expand",
    )(ckv, kpe, wk, wv)


_NT = (((1,), (1,)), ((), ()))


def _attn_prompt_kernel(q_ref, k_ref, v_ref, o_ref, *, tile):
    i = pl.program_id(2)
    q = q_ref[...]

    def step(j, carry, masked):
        m, l, acc = carry
        off = pl.multiple_of(j * tile, tile)
        s = lax.dot_general(q, k_ref[pl.ds(off, tile), :], _NT, preferred_element_type=F32)
        if masked:
            qc = lax.broadcasted_iota(jnp.int32, s.shape, 0) // CHUNK
            kc = lax.broadcasted_iota(jnp.int32, s.shape, 1) // CHUNK
            s = jnp.where(kc <= qc, s, -jnp.inf)
        m_new = jnp.maximum(m, jnp.max(s, axis=-1, keepdims=True))
        alpha = jnp.exp(m - m_new)
        p = jnp.exp(s - m_new)
        l = alpha * l + jnp.sum(p, axis=-1, keepdims=True)
        acc = alpha * acc + jnp.dot(p.astype(BF16), v_ref[pl.ds(off, tile), :],
                                    preferred_element_type=F32)
        return m_new, l, acc

    init = (jnp.full((tile, 1), -jnp.inf, F32), jnp.zeros((tile, 1), F32),
            jnp.zeros((tile, MLA_V), F32))
    carry = lax.fori_loop(0, i, functools.partial(step, masked=False), init)
    _, l, acc = step(i, carry, True)
    o_ref[...] = (acc / l).astype(o_ref.dtype)


def _attn_prompt(q, k, v, batch, seq):
    t = ATTN_TILE
    nq = seq // t
    return pl.pallas_call(
        functools.partial(_attn_prompt_kernel, tile=t),
        grid=(batch, MLA_HEADS, nq),
        in_specs=[pl.BlockSpec((t, HEAD_K), lambda b, h, i: (b * nq + i, h)),
                  pl.BlockSpec((seq, HEAD_K), lambda b, h, i: (b, h)),
                  pl.BlockSpec((seq, MLA_V), lambda b, h, i: (b, h))],
        out_specs=pl.BlockSpec((t, MLA_V), lambda b, h, i: (b * nq + i, h)),
        out_shape=jax.ShapeDtypeStruct((batch * seq, MLA_HEADS * MLA_V), BF16),
        compiler_params=_cparams("arbitrary", "arbitrary", "arbitrary"),
        name="attn_prompt",
    )(q, k, v)


def _attn_sample_kernel(q_ref, kp_ref, vp_ref, kn_ref, vn_ref, o_ref, *, n_new):
    q = q_ref[...]
    s1 = lax.dot_general(q, kp_ref[...], _NT, preferred_element_type=F32)
    s2 = lax.dot_general(q, kn_ref[...], _NT, preferred_element_type=F32)
    s2 = jnp.where(lax.broadcasted_iota(jnp.int32, s2.shape, 1) < n_new, s2, -jnp.inf)
    m = jnp.maximum(jnp.max(s1, axis=-1, keepdims=True), jnp.max(s2, axis=-1, keepdims=True))
    p1 = jnp.exp(s1 - m)
    p2 = jnp.exp(s2 - m)
    l = jnp.sum(p1, axis=-1, keepdims=True) + jnp.sum(p2, axis=-1, keepdims=True)
    acc = jnp.dot(p1.astype(BF16), vp_ref[...], preferred_element_type=F32)
    acc = acc + jnp.dot(p2.astype(BF16), vn_ref[...], preferred_element_type=F32)
    o_ref[...] = (acc / l).astype(o_ref.dtype)


def _attn_sample(q, k_past, v_past, k, v, dec_batch, past, n_new, first_chunk):
    c = SGU_CHUNK
    new = lambda w: pl.BlockSpec((c, w), lambda b, h: (first_chunk + b, h))
    old = lambda w: pl.BlockSpec((past, w), lambda b, h: (b, h))
    return pl.pallas_call(
        functools.partial(_attn_sample_kernel, n_new=n_new),
        grid=(dec_batch, MLA_HEADS),
        in_specs=[new(HEAD_K), old(HEAD_K), old(MLA_V), new(HEAD_K), new(MLA_V)],
        out_specs=pl.BlockSpec((c, MLA_V), lambda b, h: (b, h)),
        out_shape=jax.ShapeDtypeStruct((dec_batch * c, MLA_HEADS * MLA_V), BF16),
        compiler_params=_cparams("arbitrary", "arbitrary"),
        name="attn_sample",
    )(q, k_past, v_past, k, v)


def _route(h, wr_ref, br_ref, n_exp):
    logits = jnp.dot(h, wr_ref[...], precision=lax.Precision.HIGHEST,
                     preferred_element_type=F32) + br_ref[...]
    lane = lax.broadcasted_iota(jnp.int32, logits.shape, 1)
    lg = jnp.where(lane < n_exp, logits, -jnp.inf)
    v1 = jnp.max(lg, axis=-1, keepdims=True)
    i1 = jnp.min(jnp.where(lg == v1, lane, LANE), axis=-1, keepdims=True)
    lg2 = jnp.where(lane == i1, -jnp.inf, lg)
    v2 = jnp.max(lg2, axis=-1, keepdims=True)
    i2 = jnp.min(jnp.where(lg2 == v2, lane, LANE), axis=-1, keepdims=True)
    e = jnp.exp(v2 - v1)
    g1 = 1.0 / (1.0 + e)
    g2 = e / (1.0 + e)
    out = jnp.where(lane == 0, i1.astype(F32), 0.0)
    out = jnp.where(lane == 1, i2.astype(F32), out)
    out = jnp.where(lane == 2, g1, out)
    return jnp.where(lane == 3, g2, out)


def _pack_bf16_pairs(h):
    w = h.shape[1] // 2
    hi = pltpu.bitcast(h[:, :w].astype(BF16).astype(F32), U32)
    lo = pltpu.bitcast(h[:, w:].astype(BF16).astype(F32), U32)
    return (hi & jnp.uint32(0xFFFF0000)) | (lo >> 16)


def _unpack_bf16_pairs(w):
    hi = pltpu.bitcast(w & jnp.uint32(0xFFFF0000), F32).astype(BF16)
    lo = pltpu.bitcast(w << 16, F32).astype(BF16)
    return hi, lo


def _out_kernel(*refs, alpha, n_exp, routed):
    if routed:
        (attn_ref, z_ref, zprev_ref, bg_ref, u_ref, v_ref, cw_ref, sw_ref, sb_ref, x_ref,
         g1_ref, sh2_ref, sc2_ref, lng_ref, lnb_ref, wo_ref, wr_ref, br_ref,
         x_out, h_out, route_out, mix_ref) = refs
    else:
        (attn_ref, z_ref, zprev_ref, bg_ref, u_ref, v_ref, cw_ref, sw_ref, sb_ref, x_ref,
         g1_ref, sh2_ref, sc2_ref, lng_ref, lnb_ref, wo_ref,
         x_out, h_out, mix_ref) = refs
    t, d = x_ref.shape
    nc = t // SGU_CHUNK
    n_attn = attn_ref.shape[1]
    conv_dim = z_ref.shape[1]

    z = z_ref[...]
    zp = zprev_ref[...]
    pos = lax.broadcasted_iota(jnp.int32, (nc, SGU_CHUNK, conv_dim), 1)
    r1 = pltpu.roll(z, 1, 0).reshape(nc, SGU_CHUNK, conv_dim)
    r2 = pltpu.roll(z, 2, 0).reshape(nc, SGU_CHUNK, conv_dim)
    r1 = jnp.where(pos == 0, zp[:, 1:2, :], r1)
    r2 = jnp.where(pos == 0, zp[:, 0:1, :], jnp.where(pos == 1, zp[:, 1:2, :], r2))
    cw = cw_ref[...]
    y = cw[0:1, :] * r2 + cw[1:2, :] * r1 + cw[2:3, :] * z.reshape(nc, SGU_CHUNK, conv_dim)
    conv = bg_ref[...] * y.reshape(t, conv_dim)
    mix_ref[:, :n_attn] = attn_ref[...]
    mix_ref[:, n_attn:n_attn + conv_dim] = conv.astype(BF16)

    tri = (lax.broadcasted_iota(jnp.int32, (SGU_CHUNK, SGU_CHUNK), 0)
           >= lax.broadcasted_iota(jnp.int32, (SGU_CHUNK, SGU_CHUNK), 1))
    hw = v_ref.shape[1] // SGU_HEADS
    for hd in range(SGU_HEADS):
        w_m = jnp.where(tri, sw_ref[hd], 0.0).astype(BF16)
        bias = sb_ref[:, hd:hd + 1]
        for c in range(nc):
            rows = slice(c * SGU_CHUNK, (c + 1) * SGU_CHUNK)
            cols = slice(hd * hw, (hd + 1) * hw)
            mixed = jnp.dot(w_m, v_ref[rows, cols].astype(BF16), preferred_element_type=F32) + bias
            mix_ref[rows, n_attn + conv_dim + hd * hw:n_attn + conv_dim + (hd + 1) * hw] = (
                u_ref[rows, cols] * mixed).astype(BF16)

    o = jnp.dot(mix_ref[...], wo_ref[...], preferred_element_type=F32)
    x3 = x_ref[...].reshape(nc, SGU_CHUNK, d)
    yres = (alpha * x3 + (1.0 + g1_ref[...]) * o.reshape(nc, SGU_CHUNK, d)).reshape(t, d)
    x1 = _ln(yres, lng_ref[...], lnb_ref[...])
    x_out[...] = x1
    h2 = (x1.reshape(nc, SGU_CHUNK, d) * (1.0 + sc2_ref[...]) + sh2_ref[...]).reshape(t, d)
    if routed:
        h_out[...] = _pack_bf16_pairs(h2)
        route_out[...] = _route(h2, wr_ref, br_ref, n_exp)
    else:
        h_out[...] = h2.astype(BF16)


def _mixer_out(attn, z, zprev, bg, u, v, conv_w, sgu_w, sgu_bt, x, mod3, lng, lnb, w_o,
               alpha, router=None):
    n, d = x.shape
    t = ROW_TILE
    nc = t // SGU_CHUNK
    routed = router is not None
    row = lambda w: pl.BlockSpec((t, w), lambda i: (i, 0))
    mod = lambda k: pl.BlockSpec((nc, 1, d), lambda i: (i, 0, k))
    in_specs = [row(attn.shape[1]), row(z.shape[1]),
                pl.BlockSpec((nc,) + zprev.shape[1:], lambda i: (i, 0, 0)),
                row(bg.shape[1]), row(u.shape[1]), row(v.shape[1]),
                _const_spec(conv_w.shape), _const_spec(sgu_w.shape), _const_spec(sgu_bt.shape),
                row(d), mod(2), mod(3), mod(4), _const_spec(lng.shape), _const_spec(lnb.shape),
                _const_spec(w_o.shape)]
    args = [attn, z, zprev, bg, u, v, conv_w, sgu_w, sgu_bt, x, mod3, mod3, mod3, lng, lnb, w_o]
    out_specs = [row(d)]
    out_shape = [jax.ShapeDtypeStruct((n, d), F32)]
    n_exp = 0
    if routed:
        w_r, b_r, n_exp = router
        in_specs += [_const_spec(w_r.shape), _const_spec(b_r.shape)]
        args += [w_r, b_r]
        out_specs += [row(d // 2), row(LANE)]
        out_shape += [jax.ShapeDtypeStruct((n, d // 2), U32), jax.ShapeDtypeStruct((n, LANE), F32)]
    else:
        out_specs += [row(d)]
        out_shape += [jax.ShapeDtypeStruct((n, d), BF16)]
    return pl.pallas_call(
        functools.partial(_out_kernel, alpha=alpha, n_exp=n_exp, routed=routed),
        grid=(n // t,),
        in_specs=in_specs,
        out_specs=out_specs,
        out_shape=out_shape,
        scratch_shapes=[pltpu.VMEM((t, d), BF16)],
        compiler_params=_cparams("arbitrary"),
        name="mixer_out_routed" if routed else "mixer_out",
    )(*args)


def _ffn_kernel(h_ref, w1_ref, w3_ref, w2_ref, x_ref, g2_ref, lng_ref, lnb_ref, o_ref, acc_ref,
                *, alpha):
    j = pl.program_id(1)

    @pl.when(j == 0)
    def _():
        acc_ref[...] = jnp.zeros_like(acc_ref)

    h = h_ref[...]
    a = jnp.dot(h, w1_ref[...], preferred_element_type=F32)
    b = jnp.dot(h, w3_ref[...], preferred_element_type=F32)
    acc_ref[...] += jnp.dot((_silu(a) * b).astype(BF16), w2_ref[...], preferred_element_type=F32)

    @pl.when(j == pl.num_programs(1) - 1)
    def _():
        t, d = x_ref.shape
        nc = t // SGU_CHUNK
        x3 = x_ref[...].reshape(nc, SGU_CHUNK, d)
        y = alpha * x3 + (1.0 + g2_ref[...]) * acc_ref[...].reshape(nc, SGU_CHUNK, d)
        o_ref[...] = _ln(y.reshape(t, d), lng_ref[...], lnb_ref[...])


def _ffn_dense(h, w1, w3, w2, x, mod3, lng, lnb, alpha):
    n, d = x.shape
    f = w1.shape[1]
    t, tf = ROW_TILE, FF_TILE
    nc = t // SGU_CHUNK
    return pl.pallas_call(
        functools.partial(_ffn_kernel, alpha=alpha),
        grid=(n // t, f // tf),
        in_specs=[pl.BlockSpec((t, d), lambda i, j: (i, 0)),
                  pl.BlockSpec((d, tf), lambda i, j: (0, j)),
                  pl.BlockSpec((d, tf), lambda i, j: (0, j)),
                  pl.BlockSpec((tf, d), lambda i, j: (j, 0)),
                  pl.BlockSpec((t, d), lambda i, j: (i, 0)),
                  pl.BlockSpec((nc, 1, d), lambda i, j: (i, 0, 5)),
                  _const_spec(lng.shape), _const_spec(lnb.shape)],
        out_specs=pl.BlockSpec((t, d), lambda i, j: (i, 0)),
        out_shape=jax.ShapeDtypeStruct((n, d), F32),
        scratch_shapes=[pltpu.VMEM((t, d), F32)],
        compiler_params=_cparams("arbitrary", "arbitrary"),
        name="ffn_dense",
    )(h, w1, w3, w2, x, mod3, lng, lnb)


def _moe_kernel(te_ref, tv_ref, rt_ref, h_hbm, w1_ref, w3_ref, w2_ref, gate_ref, o_ref,
                hbuf, hb, sem, *, tm, n_tiles):
    r = pl.program_id(0)
    j = pl.program_id(1)
    half = hb.shape[1] // 2

    def row_copy(tile, slot, i):
        tok = rt_ref[tile * tm + i]
        return pltpu.make_async_copy(h_hbm.at[pl.ds(tok, 1), :], hbuf.at[slot, pl.ds(i, 1), :],
                                     sem.at[slot])

    def start_tile(tile, slot):
        def body(i, c):
            row_copy(tile, slot, i).start()
            return c
        lax.fori_loop(0, tm, body, 0)

    def wait_tile(tile, slot):
        def body(i, c):
            row_copy(tile, slot, i).wait()
            return c
        lax.fori_loop(0, tm, body, 0)

    slot = r % 2

    @pl.when(j == 0)
    def _():
        @pl.when(r == 0)
        def _():
            start_tile(0, 0)

        wait_tile(r, slot)

        @pl.when(r + 1 < n_tiles)
        def _():
            start_tile(r + 1, 1 - slot)

        hi, lo = _unpack_bf16_pairs(hbuf[slot])
        hb[:, :half] = hi
        hb[:, half:] = lo
        o_ref[...] = jnp.zeros_like(o_ref)

    @pl.when(tv_ref[r] == 1)
    def _():
        h = hb[...]
        a = jnp.dot(h, w1_ref[...], preferred_element_type=F32)
        b = jnp.dot(h, w3_ref[...], preferred_element_type=F32)
        o_ref[...] += jnp.dot((_silu(a) * b).astype(BF16), w2_ref[...],
                              preferred_element_type=F32)

    @pl.when(j == pl.num_programs(1) - 1)
    def _():
        o_ref[...] = o_ref[...] * gate_ref[...]


def _moe_experts(h_packed, w1, w3, w2, tile_expert, tile_valid, row_token, row_gate):
    n_exp, d, f = w1.shape
    p = row_token.shape[0]
    tm, tf = ROW_TILE, FF_TILE
    n_tiles = p // tm
    nj = f // tf
    col = lambda j, tv, r: j * tv[r] + (nj - 1) * (1 - tv[r])
    grid_spec = pltpu.PrefetchScalarGridSpec(
        num_scalar_prefetch=3,
        grid=(n_tiles, nj),
        in_specs=[pl.BlockSpec(memory_space=pl.ANY),
                  pl.BlockSpec((None, d, tf), lambda r, j, te, tv, rt: (te[r], 0, col(j, tv, r))),
                  pl.BlockSpec((None, d, tf), lambda r, j, te, tv, rt: (te[r], 0, col(j, tv, r))),
                  pl.BlockSpec((None, tf, d), lambda r, j, te, tv, rt: (te[r], col(j, tv, r), 0)),
                  pl.BlockSpec((tm, 1), lambda r, j, te, tv, rt: (r, 0))],
        out_specs=pl.BlockSpec((tm, d), lambda r, j, te, tv, rt: (r, 0)),
        scratch_shapes=[pltpu.VMEM((2, tm, d // 2), U32), pltpu.VMEM((tm, d), BF16),
                        pltpu.SemaphoreType.DMA((2,))],
    )
    return pl.pallas_call(
        functools.partial(_moe_kernel, tm=tm, n_tiles=n_tiles),
        grid_spec=grid_spec,
        out_shape=jax.ShapeDtypeStruct((p, d), F32),
        compiler_params=_cparams("arbitrary", "arbitrary"),
        name="moe_experts",
    )(tile_expert, tile_valid, row_token, h_packed, w1, w3, w2, row_gate.reshape(p, 1))


def _combine_kernel(pos_ref, ys_hbm, x_ref, g2_ref, lng_ref, lnb_ref, o_ref, ybuf, sem,
                    *, tc, n_tiles, alpha):
    i = pl.program_id(0)

    def row_copy(tile, slot, k, r):
        p = pos_ref[(tile * tc + r) * TOP_K + k]
        return pltpu.make_async_copy(ys_hbm.at[pl.ds(p, 1), :], ybuf.at[slot, k, pl.ds(r, 1), :],
                                     sem.at[slot])

    def start_tile(tile, slot):
        def body(r, c):
            for k in range(TOP_K):
                row_copy(tile, slot, k, r).start()
            return c
        lax.fori_loop(0, tc, body, 0)

    def wait_tile(tile, slot):
        def body(r, c):
            for k in range(TOP_K):
                row_copy(tile, slot, k, r).wait()
            return c
        lax.fori_loop(0, tc, body, 0)

    slot = i % 2

    @pl.when(i == 0)
    def _():
        start_tile(0, 0)

    wait_tile(i, slot)

    @pl.when(i + 1 < n_tiles)
    def _():
        start_tile(i + 1, 1 - slot)

    t, d = x_ref.shape
    nc = t // SGU_CHUNK
    f = ybuf[slot, 0]
    for k in range(1, TOP_K):
        f = f + ybuf[slot, k]
    x3 = x_ref[...].reshape(nc, SGU_CHUNK, d)
    y = alpha * x3 + (1.0 + g2_ref[...]) * f.reshape(nc, SGU_CHUNK, d)
    o_ref[...] = _ln(y.reshape(t, d), lng_ref[...], lnb_ref[...])


def _moe_combine(ys, pos, x, mod3, lng, lnb, alpha):
    n, d = x.shape
    tc = COMBINE_TILE
    nc = tc // SGU_CHUNK
    n_tiles = n // tc
    grid_spec = pltpu.PrefetchScalarGridSpec(
        num_scalar_prefetch=1,
        grid=(n_tiles,),
        in_specs=[pl.BlockSpec(memory_space=pl.ANY),
                  pl.BlockSpec((tc, d), lambda i, pos: (i, 0)),
                  pl.BlockSpec((nc, 1, d), lambda i, pos: (i, 0, 5)),
                  _const_spec(lng.shape), _const_spec(lnb.shape)],
        out_specs=pl.BlockSpec((tc, d), lambda i, pos: (i, 0)),
        scratch_shapes=[pltpu.VMEM((2, TOP_K, tc, d), F32), pltpu.SemaphoreType.DMA((2,))],
    )
    return pl.pallas_call(
        functools.partial(_combine_kernel, tc=tc, n_tiles=n_tiles, alpha=alpha),
        grid_spec=grid_spec,
        out_shape=jax.ShapeDtypeStruct((n, d), F32),
        compiler_params=_cparams("arbitrary"),
        name="moe_combine",
    )(pos.reshape(-1), ys, x, mod3, lng, lnb)


def _dispatch_tables(route, real_rows, n_rows, n_exp, tm):
    n_real = real_rows.shape[0]
    sel = route[real_rows, :2 * TOP_K]
    e = sel[:, :TOP_K].astype(jnp.int32).reshape(-1)
    g = sel[:, TOP_K:].reshape(-1)
    onehot = (e[:, None] == jnp.arange(n_exp, dtype=jnp.int32)[None, :]).astype(jnp.int32)
    csum = jnp.cumsum(onehot, axis=0)
    rank = jnp.take_along_axis(csum, e[:, None], axis=1)[:, 0] - 1
    counts = csum[-1]
    padded = ((counts + tm - 1) // tm) * tm
    ends = jnp.cumsum(padded)
    dest = (ends - padded)[e] + rank
    p = -(-(TOP_K * n_real) // tm) * tm + n_exp * tm
    tok = jnp.repeat(jnp.asarray(real_rows, jnp.int32), TOP_K)
    row_token = jnp.zeros((p,), jnp.int32).at[dest].set(tok)
    row_gate = jnp.zeros((p,), F32).at[dest].set(g)
    tile_start = jnp.arange(p // tm, dtype=jnp.int32) * tm
    tile_expert = jnp.minimum(jnp.sum(tile_start[:, None] >= ends[None, :], axis=1), n_exp - 1)
    tile_valid = (tile_start < ends[-1]).astype(jnp.int32)
    pos = jnp.zeros((n_rows, TOP_K), jnp.int32).at[real_rows].set(dest.reshape(n_real, TOP_K))
    return tile_expert.astype(jnp.int32), tile_valid, row_token, row_gate, pos


def _rot_cols(w):
    half = w.shape[-1] // 2
    return jnp.concatenate([-w[..., half:], w[..., :half]], axis=-1)


def _pad_cols(w, width):
    return jnp.pad(w, ((0, 0), (0, width - w.shape[1])))


def _layout_w_in(w, q_lora, kv_lora):
    o = q_lora + kv_lora
    kpe = w[:, o:o + MLA_ROPE]
    return jnp.concatenate([w[:, :o], _pad_cols(kpe, LANE), _pad_cols(_rot_cols(kpe), LANE),
                            w[:, o + MLA_ROPE:]], axis=1).astype(BF16)


def _layout_w_uq(w):
    per = MLA_NOPE + MLA_ROPE
    main, rot = [], []
    for hd in range(MLA_HEADS):
        pe = w[:, hd * per + MLA_NOPE:(hd + 1) * per]
        main += [w[:, hd * per:hd * per + MLA_NOPE], _pad_cols(pe, LANE)]
        rot += [_pad_cols(_rot_cols(pe), LANE)]
    return jnp.concatenate(main, axis=1).astype(BF16), jnp.concatenate(rot, axis=1).astype(BF16)


def _layout_w_ukv(w):
    per = MLA_NOPE + MLA_V
    wk = jnp.concatenate([w[:, hd * per:hd * per + MLA_NOPE] for hd in range(MLA_HEADS)], axis=1)
    wv = jnp.concatenate([w[:, hd * per + MLA_NOPE:(hd + 1) * per] for hd in range(MLA_HEADS)], axis=1)
    return wk.astype(BF16), wv.astype(BF16)


def _rope_table(pos):
    half = MLA_ROPE // 2
    inv = ROPE_THETA ** (-jnp.arange(half, dtype=F32) / half)
    ang = pos.astype(F32)[:, None] * inv[None, :]
    pad = jnp.zeros((pos.shape[0], LANE - MLA_ROPE), F32)
    cos, sin = jnp.cos(ang), jnp.sin(ang)
    return jnp.concatenate([cos, cos, pad, sin, sin, pad], axis=1)


def kernel(x_prompt, x_sample, cache_ckv, cache_kpe, state_conv, c_prompt, c_sample, w_ada, b_ada, w_in, q_norm_g, kv_norm_g, w_uq, w_ukv, conv_w, sgu_ln_g, sgu_ln_b, sgu_w, sgu_b, w_o, ln_g, ln_b, ffn_w1, ffn_w3, ffn_w2, router_w, router_b, exp_w1, exp_w3, exp_w2):
    batch, seq, d = x_prompt.shape
    dec_batch, dec_seq, _ = x_sample.shape
    depth, _, past, kv_lora = cache_ckv.shape
    q_lora = q_norm_g.shape[1]
    conv_dim = conv_w.shape[2]
    sgu_dim = sgu_ln_g.shape[1]
    n_exp = router_w.shape[2]
    c = SGU_CHUNK
    assert seq % ATTN_TILE == 0 and (batch * seq) % ROW_TILE == 0
    assert CONV_WIDTH - 1 <= dec_seq <= c and (dec_batch * c) % ROW_TILE == 0
    assert sgu_dim == SGU_HEADS * LANE and (dec_batch * past) % ROW_TILE == 0
    alpha = (2 * depth) ** 0.25

    n_p = batch * seq
    n_s = dec_batch * c
    n = n_p + n_s
    n_chunks = n // c
    chunk_batch = np.concatenate([np.arange(n_p // c) // (seq // c), batch + np.arange(dec_batch)])
    real_rows = np.concatenate([np.arange(n_p),
                                (n_p + c * np.arange(dec_batch)[:, None] + np.arange(dec_seq)[None, :]).reshape(-1)])
    pos = jnp.concatenate([jnp.tile(jnp.arange(seq), batch),
                           jnp.tile(past + jnp.arange(c), dec_batch)])
    cs = _rope_table(pos)

    x = jnp.concatenate([x_prompt.reshape(n_p, d),
                         jnp.pad(x_sample, ((0, 0), (0, c - dec_seq), (0, 0))).reshape(n_s, d)], axis=0)
    n_c = batch + dec_batch
    c_all = jnp.pad(jnp.concatenate([c_prompt, c_sample], axis=0), ((0, -n_c % 8), (0, 0)))
    mod = _adaln_mod(c_all, w_ada, b_ada)

    outs = {k: [] for k in ("ckv_p", "kpe_p", "conv_p", "ckv_s", "kpe_s", "conv_s", "v_s")}
    keep = min(seq, past)
    for l in range(depth):
        mod3 = mod[l][chunk_batch].reshape(n_chunks, 1, 6 * d)
        w_in_p = _layout_w_in(w_in[l], q_lora, kv_lora)
        wq_p, wq_rot = _layout_w_uq(w_uq[l])
        wk, wv = _layout_w_ukv(w_ukv[l])
        row2 = lambda a: a.reshape(1, -1)

        q, ckv, kpe, z, bg, u, v = _in_proj(
            x, mod3, cs, w_in_p, row2(q_norm_g[l]), row2(kv_norm_g[l]), wq_p, wq_rot,
            row2(sgu_ln_g[l]), row2(sgu_ln_b[l]))

        k_new, v_new = _kv_expand(ckv, kpe, wk, wv)
        k_past, v_past = _kv_expand(cache_ckv[l].reshape(dec_batch * past, kv_lora),
                                    _pad_cols(cache_kpe[l].reshape(dec_batch * past, MLA_ROPE), LANE),
                                    wk, wv)
        attn = jnp.concatenate([
            _attn_prompt(q, k_new, v_new, batch, seq),
            _attn_sample(q, k_past, v_past, k_new, v_new, dec_batch, past, dec_seq, n_p // c)], axis=0)

        z_p = z[:n_p].reshape(batch, seq // c, c, conv_dim)[:, :, c - (CONV_WIDTH - 1):, :]
        zprev = jnp.concatenate([
            jnp.concatenate([jnp.zeros_like(z_p[:, :1]), z_p[:, :-1]], axis=1).reshape(n_p // c, CONV_WIDTH - 1, conv_dim),
            state_conv[l]], axis=0)

        moe = l % 2 == 1
        i = l // 2
        router = None
        if moe:
            router = (_pad_cols(router_w[i], LANE), _pad_cols(router_b[i].reshape(1, n_exp), LANE), n_exp)
        res = _mixer_out(attn, z, zprev, bg, u, v, conv_w[l], sgu_w[l].astype(F32), sgu_b[l].T,
                         x, mod3, row2(ln_g[l, 0]), row2(ln_b[l, 0]), w_o[l].astype(BF16), alpha,
                         router=router)
        if moe:
            x1, h2, route = res
            te, tv, row_token, row_gate, rpos = _dispatch_tables(route, real_rows, n, n_exp, ROW_TILE)
            ys = _moe_experts(h2, exp_w1[i].astype(BF16), exp_w3[i].astype(BF16),
                              exp_w2[i].astype(BF16), te, tv, row_token, row_gate)
            x = _moe_combine(ys, rpos, x1, mod3, row2(ln_g[l, 1]), row2(ln_b[l, 1]), alpha)
        else:
            x1, h2 = res
            x = _ffn_dense(h2, ffn_w1[i].astype(BF16), ffn_w3[i].astype(BF16),
                           ffn_w2[i].astype(BF16), x1, mod3, row2(ln_g[l, 1]), row2(ln_b[l, 1]), alpha)

        kpe = kpe[:, :MLA_ROPE]
        sample = lambda a: a[n_p:].reshape(dec_batch, c, a.shape[1])[:, :dec_seq]
        outs["ckv_p"].append(ckv[:n_p].reshape(batch, seq, kv_lora)[:, seq - keep:])
        outs["kpe_p"].append(kpe[:n_p].reshape(batch, seq, MLA_ROPE)[:, seq - keep:])
        outs["conv_p"].append(z[:n_p].reshape(batch, seq, conv_dim)[:, seq - (CONV_WIDTH - 1):])
        outs["ckv_s"].append(sample(ckv))
        outs["kpe_s"].append(sample(kpe))
        outs["conv_s"].append(sample(z)[:, dec_seq - (CONV_WIDTH - 1):])
        outs["v_s"].append(sample(v))

    y_p = x[:n_p].reshape(batch, seq, d)
    y_s = x[n_p:].reshape(dec_batch, c, d)[:, :dec_seq]
    st = lambda k: jnp.stack(outs[k])
    return (y_p, y_s, st("ckv_p"), st("kpe_p"), st("conv_p"), st("ckv_s"), st("kpe_s"),
            st("conv_s"), st("v_s"))
```

```python
import functools

import numpy as np
import jax
import jax.numpy as jnp
from jax import lax
from jax.experimental import pallas as pl
from jax.experimental.pallas import tpu as pltpu

F32 = jnp.float32
BF16 = jnp.bfloat16
U32 = jnp.uint32

CHUNK = 64
MLA_HEADS = 8
MLA_NOPE = 128
MLA_ROPE = 64
MLA_V = 128
ROPE_THETA = 10000.0
ATTN_SCALE = (MLA_NOPE + MLA_ROPE) ** -0.5
Q_SCALE = ATTN_SCALE * float(np.log2(np.e))
CONV_WIDTH = 3
SGU_HEADS = 4
SGU_CHUNK = 128
TOP_K = 2
LN_EPS = 1e-5
RMS_EPS = 1e-6

LANE = 128
VMEM_LIMIT = 56 * 2 ** 20

HEAD_K = 2 * LANE
HEAD_V = 2 * LANE
HEADS_PER_STEP = 2
ROW_TILE = 512
FF_TILE = 512
ATTN_TILE = 512
COMBINE_TILE = 256


def _cparams(*sem):
    return pltpu.CompilerParams(dimension_semantics=sem, vmem_limit_bytes=VMEM_LIMIT)


def _const_spec(shape):
    nd = len(shape)
    return pl.BlockSpec(shape, lambda *_: (0,) * nd, pipeline_mode=pl.Buffered(1))


def _rms(x, g):
    return x * lax.rsqrt(jnp.mean(x * x, axis=-1, keepdims=True) + RMS_EPS) * g


def _ln(x, g, b):
    mu = jnp.mean(x, axis=-1, keepdims=True)
    xc = x - mu
    var = jnp.mean(xc * xc, axis=-1, keepdims=True)
    return xc * lax.rsqrt(var + LN_EPS) * g + b


def _gelu(x):
    return 0.5 * x * (1.0 + lax.erf(x * (2.0 ** -0.5)))


def _silu(x):
    return x * jax.nn.sigmoid(x)


def _per_chunk(x, fn):
    t, w = x.shape
    return fn(x.reshape(t // SGU_CHUNK, SGU_CHUNK, w)).reshape(t, w)


def _mod_kernel(c_ref, w_ref, b_ref, o_ref):
    a = _silu(c_ref[...]).astype(BF16)
    o_ref[...] = jnp.dot(a, w_ref[...].astype(BF16), preferred_element_type=F32) + b_ref[...]


def _adaln_mod(c_all, w_ada, b_ada):
    depth, d, m = w_ada.shape
    r = c_all.shape[0]
    tn = m // 8
    return pl.pallas_call(
        _mod_kernel,
        grid=(depth, m // tn),
        in_specs=[pl.BlockSpec((r, d), lambda l, j: (0, 0)),
                  pl.BlockSpec((None, d, tn), lambda l, j: (l, 0, j)),
                  pl.BlockSpec((None, 1, tn), lambda l, j: (l, 0, j))],
        out_specs=pl.BlockSpec((None, r, tn), lambda l, j: (l, 0, j)),
        out_shape=jax.ShapeDtypeStruct((depth, r, m), F32),
        compiler_params=_cparams("arbitrary", "arbitrary"),
        name="adaln_mod",
    )(c_all, w_ada, b_ada.reshape(depth, 1, m))


def _in_kernel(x_ref, sh_ref, sc_ref, cs_ref, w_in_ref, qg_ref, kvg_ref, wq_ref, wqr_ref,
               lng_ref, lnb_ref, q_out, ckv_out, kpe_out, z_out, bg_out, u_out, v_out,
               *, q_lora, kv_lora, conv_dim, sgu_dim):
    t, d = x_ref.shape
    nc = t // SGU_CHUNK
    x3 = x_ref[...].reshape(nc, SGU_CHUNK, d)
    h = (x3 * (1.0 + sc_ref[...]) + sh_ref[...]).reshape(t, d)
    proj = jnp.dot(h.astype(BF16), w_in_ref[...], preferred_element_type=F32)

    o = 0
    q_lat = proj[:, o:o + q_lora]; o += q_lora
    kv_lat = proj[:, o:o + kv_lora]; o += kv_lora
    kpe_raw = proj[:, o:o + LANE]; o += LANE
    kpe_rot = proj[:, o:o + LANE]; o += LANE
    b_g = proj[:, o:o + conv_dim]; o += conv_dim
    c_g = proj[:, o:o + conv_dim]; o += conv_dim
    h_c = proj[:, o:o + conv_dim]; o += conv_dim
    u = proj[:, o:o + sgu_dim]; o += sgu_dim
    v = proj[:, o:o + sgu_dim]

    cos = cs_ref[:, :LANE]
    sin = cs_ref[:, LANE:]
    ckv_out[...] = _rms(kv_lat, kvg_ref[...])
    kpe_out[...] = kpe_raw * cos + kpe_rot * sin

    qn = _rms(q_lat, qg_ref[...]).astype(BF16)
    raw = jnp.dot(qn, wq_ref[...], preferred_element_type=F32)
    rot = jnp.dot(qn, wqr_ref[...], preferred_element_type=F32)
    for hd in range(MLA_HEADS):
        a = hd * HEAD_K
        q_out[:, a:a + LANE] = (raw[:, a:a + LANE] * Q_SCALE).astype(BF16)
        pe = raw[:, a + LANE:a + HEAD_K] * cos + rot[:, hd * LANE:(hd + 1) * LANE] * sin
        q_out[:, a + LANE:a + HEAD_K] = (pe * Q_SCALE).astype(BF16)

    bg_out[...] = b_g
    z_out[...] = c_g * h_c
    u_out[...] = _gelu(u)
    v_out[...] = _ln(_gelu(v), lng_ref[...], lnb_ref[...])


def _in_proj(x, mod3, cs, w_in_p, qg, kvg, wq_p, wq_rot, lng, lnb):
    n, d = x.shape
    q_lora, kv_lora = qg.shape[1], kvg.shape[1]
    sgu_dim = lng.shape[1]
    conv_dim = (w_in_p.shape[1] - q_lora - kv_lora - 2 * LANE - 2 * sgu_dim) // 3
    t = ROW_TILE
    nc = t // SGU_CHUNK
    row = lambda w: pl.BlockSpec((t, w), lambda i: (i, 0))
    mod = lambda k: pl.BlockSpec((nc, 1, d), lambda i: (i, 0, k))
    kern = functools.partial(_in_kernel, q_lora=q_lora, kv_lora=kv_lora, conv_dim=conv_dim,
                             sgu_dim=sgu_dim)
    widths = (MLA_HEADS * HEAD_K, kv_lora, LANE, conv_dim, conv_dim, sgu_dim, sgu_dim)
    dtypes = (BF16, F32, F32, F32, F32, F32, F32)
    return pl.pallas_call(
        kern,
        grid=(n // t,),
        in_specs=[row(d), mod(0), mod(1), row(2 * LANE), _const_spec(w_in_p.shape),
                  _const_spec(qg.shape), _const_spec(kvg.shape), _const_spec(wq_p.shape),
                  _const_spec(wq_rot.shape), _const_spec(lng.shape), _const_spec(lnb.shape)],
        out_specs=[row(w) for w in widths],
        out_shape=[jax.ShapeDtypeStruct((n, w), dt) for w, dt in zip(widths, dtypes)],
        compiler_params=_cparams("arbitrary"),
        name="in_proj",
    )(x, mod3, mod3, cs, w_in_p, qg, kvg, wq_p, wq_rot, lng, lnb)


def _kv_kernel(ckv_ref, kpe_ref, wk_ref, wv_ref, k_out, v_out):
    c = ckv_ref[...].astype(BF16)
    kn = jnp.dot(c, wk_ref[...], preferred_element_type=F32)
    vv = jnp.dot(c, wv_ref[...], preferred_element_type=F32)
    kpe = kpe_ref[...].astype(BF16)
    ones = jnp.ones((c.shape[0], HEAD_V - MLA_V), BF16)
    for hd in range(MLA_HEADS):
        a = hd * HEAD_K
        k_out[:, a:a + LANE] = kn[:, hd * MLA_NOPE:(hd + 1) * MLA_NOPE].astype(BF16)
        k_out[:, a + LANE:a + HEAD_K] = kpe
        b = hd * HEAD_V
        v_out[:, b:b + MLA_V] = vv[:, hd * MLA_V:(hd + 1) * MLA_V].astype(BF16)
        v_out[:, b + MLA_V:b + HEAD_V] = ones


def _kv_expand(ckv, kpe, wk, wv):
    n, kv_lora = ckv.shape
    t = ROW_TILE
    row = lambda w: pl.BlockSpec((t, w), lambda i: (i, 0))
    return pl.pallas_call(
        _kv_kernel,
        grid=(n // t,),
        in_specs=[row(kv_lora), row(LANE), _const_spec(wk.shape), _const_spec(wv.shape)],
        out_specs=[row(MLA_HEADS * HEAD_K), row(MLA_HEADS * HEAD_V)],
        out_shape=[jax.ShapeDtypeStruct((n, MLA_HEADS * HEAD_K), BF16),
                   jax.ShapeDtypeStruct((n, MLA_HEADS * HEAD_V), BF16)],
        compiler_params=_cparams("arbitrary"),
        name="kv_expand",
    )(ckv, kpe, wk, wv)


_NT = (((1,), (1,)), ((), ()))


def _attn_prompt_kernel(q_ref, k_ref, v_ref, o_ref, s0_ref, s1_ref, m_ref, acc_ref, *, tile):
    i = pl.program_id(2)
    heads = range(HEADS_PER_STEP)

    def scores(s_ref, j):
        off = pl.multiple_of(j * tile, tile)
        for g in heads:
            cols = slice(g * HEAD_K, (g + 1) * HEAD_K)
            s_ref[g] = lax.dot_general(q_ref[:, cols], k_ref[pl.ds(off, tile), cols], _NT,
                                       preferred_element_type=F32)

    def accumulate(s_ref, j, masked):
        off = pl.multiple_of(j * tile, tile)
        if masked:
            visible = (lax.broadcasted_iota(jnp.int32, (tile, tile), 1) // CHUNK
                       <= lax.broadcasted_iota(jnp.int32, (tile, tile), 0) // CHUNK)
        for g in heads:
            s = s_ref[g]
            if masked:
                s = jnp.where(visible, s, -jnp.inf)
            m_old = m_ref[g]
            m_new = jnp.maximum(m_old, jnp.max(s, axis=-1, keepdims=True))
            p = jnp.exp2(s - m_new).astype(BF16)
            pv = jnp.dot(p, v_ref[pl.ds(off, tile), g * HEAD_V:(g + 1) * HEAD_V],
                         preferred_element_type=F32)
            acc_ref[g] = jnp.exp2(m_old - m_new) * acc_ref[g] + pv
            m_ref[g] = m_new

    m_ref[...] = jnp.full(m_ref.shape, -jnp.inf, F32)
    acc_ref[...] = jnp.zeros(acc_ref.shape, F32)
    scores(s0_ref, 0)

    def pair(jj, c):
        scores(s1_ref, 2 * jj + 1)
        accumulate(s0_ref, 2 * jj, False)
        scores(s0_ref, 2 * jj + 2)
        accumulate(s1_ref, 2 * jj + 1, False)
        return c

    lax.fori_loop(0, i // 2, pair, 0)

    @pl.when(i % 2 == 0)
    def _():
        accumulate(s0_ref, i, True)

    @pl.when(i % 2 == 1)
    def _():
        scores(s1_ref, i)
        accumulate(s0_ref, i - 1, False)
        accumulate(s1_ref, i, True)

    for g in heads:
        acc = acc_ref[g]
        o_ref[:, g * MLA_V:(g + 1) * MLA_V] = (acc[:, :MLA_V] / acc[:, MLA_V:]).astype(o_ref.dtype)


def _attn_prompt(q, k, v, batch, seq):
    t = ATTN_TILE
    nq = seq // t
    g = HEADS_PER_STEP
    return pl.pallas_call(
        functools.partial(_attn_prompt_kernel, tile=t),
        grid=(batch, MLA_HEADS // g, nq),
        in_specs=[pl.BlockSpec((t, g * HEAD_K), lambda b, h, i: (b * nq + i, h)),
                  pl.BlockSpec((seq, g * HEAD_K), lambda b, h, i: (b, h)),
                  pl.BlockSpec((seq, g * HEAD_V), lambda b, h, i: (b, h))],
        out_specs=pl.BlockSpec((t, g * MLA_V), lambda b, h, i: (b * nq + i, h)),
        out_shape=jax.ShapeDtypeStruct((batch * seq, MLA_HEADS * MLA_V), BF16),
        scratch_shapes=[pltpu.VMEM((g, t, t), F32), pltpu.VMEM((g, t, t), F32),
                        pltpu.VMEM((g, t, 1), F32), pltpu.VMEM((g, t, HEAD_V), F32)],
        compiler_params=_cparams("arbitrary", "arbitrary", "arbitrary"),
        name="attn_prompt",
    )(q, k, v)


def _attn_sample_kernel(q_ref, kp_ref, vp_ref, kn_ref, vn_ref, o_ref, *, n_new):
    q = q_ref[...]
    s1 = lax.dot_general(q, kp_ref[...], _NT, preferred_element_type=F32)
    s2 = lax.dot_general(q, kn_ref[...], _NT, preferred_element_type=F32)
    s2 = jnp.where(lax.broadcasted_iota(jnp.int32, s2.shape, 1) < n_new, s2, -jnp.inf)
    m = jnp.maximum(jnp.max(s1, axis=-1, keepdims=True), jnp.max(s2, axis=-1, keepdims=True))
    acc = jnp.dot(jnp.exp2(s1 - m).astype(BF16), vp_ref[...], preferred_element_type=F32)
    acc = acc + jnp.dot(jnp.exp2(s2 - m).astype(BF16), vn_ref[...], preferred_element_type=F32)
    o_ref[...] = (acc[:, :MLA_V] / acc[:, MLA_V:]).astype(o_ref.dtype)


def _attn_sample(q, k_past, v_past, k, v, dec_batch, past, n_new, first_chunk):
    c = SGU_CHUNK
    new = lambda w: pl.BlockSpec((c, w), lambda b, h: (first_chunk + b, h))
    old = lambda w: pl.BlockSpec((past, w), lambda b, h: (b, h))
    return pl.pallas_call(
        functools.partial(_attn_sample_kernel, n_new=n_new),
        grid=(dec_batch, MLA_HEADS),
        in_specs=[new(HEAD_K), old(HEAD_K), old(HEAD_V), new(HEAD_K), new(HEAD_V)],
        out_specs=pl.BlockSpec((c, MLA_V), lambda b, h: (b, h)),
        out_shape=jax.ShapeDtypeStruct((dec_batch * c, MLA_HEADS * MLA_V), BF16),
        compiler_params=_cparams("arbitrary", "arbitrary"),
        name="attn_sample",
    )(q, k_past, v_past, k, v)


def _route(h, wr_ref, br_ref, n_exp):
    hi = h.astype(BF16)
    lo = (h - hi.astype(F32)).astype(BF16)
    a = jnp.dot(hi, wr_ref[...], preferred_element_type=F32)
    b = jnp.dot(lo, wr_ref[:, :LANE], preferred_element_type=F32)
    logits = a[:, :LANE] + (a[:, LANE:] + b) + br_ref[...]
    lane = lax.broadcasted_iota(jnp.int32, logits.shape, 1)
    lg = jnp.where(lane < n_exp, logits, -jnp.inf)
    v1 = jnp.max(lg, axis=-1, keepdims=True)
    i1 = jnp.min(jnp.where(lg == v1, lane, LANE), axis=-1, keepdims=True)
    lg2 = jnp.where(lane == i1, -jnp.inf, lg)
    v2 = jnp.max(lg2, axis=-1, keepdims=True)
    i2 = jnp.min(jnp.where(lg2 == v2, lane, LANE), axis=-1, keepdims=True)
    e = jnp.exp(v2 - v1)
    g1 = 1.0 / (1.0 + e)
    g2 = e / (1.0 + e)
    out = jnp.where(lane == 0, i1.astype(F32), 0.0)
    out = jnp.where(lane == 1, i2.astype(F32), out)
    out = jnp.where(lane == 2, g1, out)
    return jnp.where(lane == 3, g2, out)


def _pack_bf16_pairs(h):
    w = h.shape[1] // 2
    hi = pltpu.bitcast(h[:, :w].astype(BF16).astype(F32), U32)
    lo = pltpu.bitcast(h[:, w:].astype(BF16).astype(F32), U32)
    return (hi & jnp.uint32(0xFFFF0000)) | (lo >> 16)


def _unpack_bf16_pairs(w):
    hi = pltpu.bitcast(w & jnp.uint32(0xFFFF0000), F32).astype(BF16)
    lo = pltpu.bitcast(w << 16, F32).astype(BF16)
    return hi, lo


def _out_kernel(*refs, alpha, n_exp, routed):
    if routed:
        (attn_ref, z_ref, zprev_ref, bg_ref, u_ref, v_ref, cw_ref, sw_ref, sb_ref, x_ref,
         g1_ref, sh2_ref, sc2_ref, lng_ref, lnb_ref, wo_ref, wr_ref, br_ref,
         x_out, h_out, route_out, mix_ref) = refs
    else:
        (attn_ref, z_ref, zprev_ref, bg_ref, u_ref, v_ref, cw_ref, sw_ref, sb_ref, x_ref,
         g1_ref, sh2_ref, sc2_ref, lng_ref, lnb_ref, wo_ref,
         x_out, h_out, mix_ref) = refs
    t, d = x_ref.shape
    nc = t // SGU_CHUNK
    n_attn = attn_ref.shape[1]
    conv_dim = z_ref.shape[1]

    z = z_ref[...]
    zp = zprev_ref[...]
    pos = lax.broadcasted_iota(jnp.int32, (nc, SGU_CHUNK, conv_dim), 1)
    r1 = pltpu.roll(z, 1, 0).reshape(nc, SGU_CHUNK, conv_dim)
    r2 = pltpu.roll(z, 2, 0).reshape(nc, SGU_CHUNK, conv_dim)
    r1 = jnp.where(pos == 0, zp[:, 1:2, :], r1)
    r2 = jnp.where(pos == 0, zp[:, 0:1, :], jnp.where(pos == 1, zp[:, 1:2, :], r2))
    cw = cw_ref[...]
    y = cw[0:1, :] * r2 + cw[1:2, :] * r1 + cw[2:3, :] * z.reshape(nc, SGU_CHUNK, conv_dim)
    conv = bg_ref[...] * y.reshape(t, conv_dim)
    mix_ref[:, :n_attn] = attn_ref[...]
    mix_ref[:, n_attn:n_attn + conv_dim] = conv.astype(BF16)

    tri = (lax.broadcasted_iota(jnp.int32, (SGU_CHUNK, SGU_CHUNK), 0)
           >= lax.broadcasted_iota(jnp.int32, (SGU_CHUNK, SGU_CHUNK), 1))
    hw = v_ref.shape[1] // SGU_HEADS
    for hd in range(SGU_HEADS):
        w_m = jnp.where(tri, sw_ref[hd], 0.0).astype(BF16)
        bias = sb_ref[:, hd:hd + 1]
        for c in range(nc):
            rows = slice(c * SGU_CHUNK, (c + 1) * SGU_CHUNK)
            cols = slice(hd * hw, (hd + 1) * hw)
            mixed = jnp.dot(w_m, v_ref[rows, cols].astype(BF16), preferred_element_type=F32) + bias
            mix_ref[rows, n_attn + conv_dim + hd * hw:n_attn + conv_dim + (hd + 1) * hw] = (
                u_ref[rows, cols] * mixed).astype(BF16)

    o = jnp.dot(mix_ref[...], wo_ref[...], preferred_element_type=F32)
    x3 = x_ref[...].reshape(nc, SGU_CHUNK, d)
    yres = (alpha * x3 + (1.0 + g1_ref[...]) * o.reshape(nc, SGU_CHUNK, d)).reshape(t, d)
    x1 = _ln(yres, lng_ref[...], lnb_ref[...])
    x_out[...] = x1
    h2 = (x1.reshape(nc, SGU_CHUNK, d) * (1.0 + sc2_ref[...]) + sh2_ref[...]).reshape(t, d)
    if routed:
        h_out[...] = _pack_bf16_pairs(h2)
        route_out[...] = _route(h2, wr_ref, br_ref, n_exp)
    else:
        h_out[...] = h2.astype(BF16)


def _mixer_out(attn, z, zprev, bg, u, v, conv_w, sgu_w, sgu_bt, x, mod3, lng, lnb, w_o,
               alpha, router=None):
    n, d = x.shape
    t = ROW_TILE
    nc = t // SGU_CHUNK
    routed = router is not None
    row = lambda w: pl.BlockSpec((t, w), lambda i: (i, 0))
    mod = lambda k: pl.BlockSpec((nc, 1, d), lambda i: (i, 0, k))
    in_specs = [row(attn.shape[1]), row(z.shape[1]),
                pl.BlockSpec((nc,) + zprev.shape[1:], lambda i: (i, 0, 0)),
                row(bg.shape[1]), row(u.shape[1]), row(v.shape[1]),
                _const_spec(conv_w.shape), _const_spec(sgu_w.shape), _const_spec(sgu_bt.shape),
                row(d), mod(2), mod(3), mod(4), _const_spec(lng.shape), _const_spec(lnb.shape),
                _const_spec(w_o.shape)]
    args = [attn, z, zprev, bg, u, v, conv_w, sgu_w, sgu_bt, x, mod3, mod3, mod3, lng, lnb, w_o]
    out_specs = [row(d)]
    out_shape = [jax.ShapeDtypeStruct((n, d), F32)]
    n_exp = 0
    if routed:
        w_r, b_r, n_exp = router
        in_specs += [_const_spec(w_r.shape), _const_spec(b_r.shape)]
        args += [w_r, b_r]
        out_specs += [row(d // 2), row(LANE)]
        out_shape += [jax.ShapeDtypeStruct((n, d // 2), U32), jax.ShapeDtypeStruct((n, LANE), F32)]
    else:
        out_specs += [row(d)]
        out_shape += [jax.ShapeDtypeStruct((n, d), BF16)]
    return pl.pallas_call(
        functools.partial(_out_kernel, alpha=alpha, n_exp=n_exp, routed=routed),
        grid=(n // t,),
        in_specs=in_specs,
        out_specs=out_specs,
        out_shape=out_shape,
        scratch_shapes=[pltpu.VMEM((t, d), BF16)],
        compiler_params=_cparams("arbitrary"),
        name="mixer_out_routed" if routed else "mixer_out",
    )(*args)


def _ffn_kernel(h_ref, w1_ref, w3_ref, w2_ref, x_ref, g2_ref, lng_ref, lnb_ref, o_ref, acc_ref,
                *, alpha):
    j = pl.program_id(1)

    @pl.when(j == 0)
    def _():
        acc_ref[...] = jnp.zeros_like(acc_ref)

    h = h_ref[...]
    a = jnp.dot(h, w1_ref[...], preferred_element_type=F32)
    b = jnp.dot(h, w3_ref[...], preferred_element_type=F32)
    acc_ref[...] += jnp.dot((_silu(a) * b).astype(BF16), w2_ref[...], preferred_element_type=F32)

    @pl.when(j == pl.num_programs(1) - 1)
    def _():
        t, d = x_ref.shape
        nc = t // SGU_CHUNK
        x3 = x_ref[...].reshape(nc, SGU_CHUNK, d)
        y = alpha * x3 + (1.0 + g2_ref[...]) * acc_ref[...].reshape(nc, SGU_CHUNK, d)
        o_ref[...] = _ln(y.reshape(t, d), lng_ref[...], lnb_ref[...])


def _ffn_dense(h, w1, w3, w2, x, mod3, lng, lnb, alpha):
    n, d = x.shape
    f = w1.shape[1]
    t, tf = ROW_TILE, FF_TILE
    nc = t // SGU_CHUNK
    return pl.pallas_call(
        functools.partial(_ffn_kernel, alpha=alpha),
        grid=(n // t, f // tf),
        in_specs=[pl.BlockSpec((t, d), lambda i, j: (i, 0)),
                  pl.BlockSpec((d, tf), lambda i, j: (0, j)),
                  pl.BlockSpec((d, tf), lambda i, j: (0, j)),
                  pl.BlockSpec((tf, d), lambda i, j: (j, 0)),
                  pl.BlockSpec((t, d), lambda i, j: (i, 0)),
                  pl.BlockSpec((nc, 1, d), lambda i, j: (i, 0, 5)),
                  _const_spec(lng.shape), _const_spec(lnb.shape)],
        out_specs=pl.BlockSpec((t, d), lambda i, j: (i, 0)),
        out_shape=jax.ShapeDtypeStruct((n, d), F32),
        scratch_shapes=[pltpu.VMEM((t, d), F32)],
        compiler_params=_cparams("arbitrary", "arbitrary"),
        name="ffn_dense",
    )(h, w1, w3, w2, x, mod3, lng, lnb)


def _moe_kernel(te_ref, tv_ref, rt_ref, h_hbm, w1_ref, w3_ref, w2_ref, gate_ref, o_ref,
                hbuf, hb, sem, *, tm, n_tiles):
    r = pl.program_id(0)
    j = pl.program_id(1)
    half = hb.shape[1] // 2

    def row_copy(tile, slot, i):
        tok = rt_ref[tile * tm + i]
        return pltpu.make_async_copy(h_hbm.at[pl.ds(tok, 1), :], hbuf.at[slot, pl.ds(i, 1), :],
                                     sem.at[slot])

    def start_tile(tile, slot):
        def body(i, c):
            row_copy(tile, slot, i).start()
            return c
        lax.fori_loop(0, tm, body, 0)

    def wait_tile(tile, slot):
        def body(i, c):
            row_copy(tile, slot, i).wait()
            return c
        lax.fori_loop(0, tm, body, 0)

    slot = r % 2

    @pl.when(j == 0)
    def _():
        @pl.when(r == 0)
        def _():
            start_tile(0, 0)

        wait_tile(r, slot)

        @pl.when(r + 1 < n_tiles)
        def _():
            start_tile(r + 1, 1 - slot)

        hi, lo = _unpack_bf16_pairs(hbuf[slot])
        hb[:, :half] = hi
        hb[:, half:] = lo
        o_ref[...] = jnp.zeros_like(o_ref)

    @pl.when(tv_ref[r] == 1)
    def _():
        h = hb[...]
        a = jnp.dot(h, w1_ref[...], preferred_element_type=F32)
        b = jnp.dot(h, w3_ref[...], preferred_element_type=F32)
        o_ref[...] += jnp.dot((_silu(a) * b).astype(BF16), w2_ref[...],
                              preferred_element_type=F32)

    @pl.when(j == pl.num_programs(1) - 1)
    def _():
        o_ref[...] = o_ref[...] * gate_ref[...]


def _moe_experts(h_packed, w1, w3, w2, tile_expert, tile_valid, row_token, row_gate):
    n_exp, d, f = w1.shape
    p = row_token.shape[0]
    tm, tf = ROW_TILE, FF_TILE
    n_tiles = p // tm
    nj = f // tf
    col = lambda j, tv, r: j * tv[r] + (nj - 1) * (1 - tv[r])
    grid_spec = pltpu.PrefetchScalarGridSpec(
        num_scalar_prefetch=3,
        grid=(n_tiles, nj),
        in_specs=[pl.BlockSpec(memory_space=pl.ANY),
                  pl.BlockSpec((None, d, tf), lambda r, j, te, tv, rt: (te[r], 0, col(j, tv, r))),
                  pl.BlockSpec((None, d, tf), lambda r, j, te, tv, rt: (te[r], 0, col(j, tv, r))),
                  pl.BlockSpec((None, tf, d), lambda r, j, te, tv, rt: (te[r], col(j, tv, r), 0)),
                  pl.BlockSpec((tm, 1), lambda r, j, te, tv, rt: (r, 0))],
        out_specs=pl.BlockSpec((tm, d), lambda r, j, te, tv, rt: (r, 0)),
        scratch_shapes=[pltpu.VMEM((2, tm, d // 2), U32), pltpu.VMEM((tm, d), BF16),
                        pltpu.SemaphoreType.DMA((2,))],
    )
    return pl.pallas_call(
        functools.partial(_moe_kernel, tm=tm, n_tiles=n_tiles),
        grid_spec=grid_spec,
        out_shape=jax.ShapeDtypeStruct((p, d), F32),
        compiler_params=_cparams("arbitrary", "arbitrary"),
        name="moe_experts",
    )(tile_expert, tile_valid, row_token, h_packed, w1, w3, w2, row_gate.reshape(p, 1))


def _combine_kernel(pos_ref, ys_hbm, x_ref, g2_ref, lng_ref, lnb_ref, o_ref, ybuf, sem,
                    *, tc, n_tiles, alpha):
    i = pl.program_id(0)

    def row_copy(tile, slot, k, r):
        p = pos_ref[(tile * tc + r) * TOP_K + k]
        return pltpu.make_async_copy(ys_hbm.at[pl.ds(p, 1), :], ybuf.at[slot, k, pl.ds(r, 1), :],
                                     sem.at[slot])

    def start_tile(tile, slot):
        def body(r, c):
            for k in range(TOP_K):
                row_copy(tile, slot, k, r).start()
            return c
        lax.fori_loop(0, tc, body, 0)

    def wait_tile(tile, slot):
        def body(r, c):
            for k in range(TOP_K):
                row_copy(tile, slot, k, r).wait()
            return c
        lax.fori_loop(0, tc, body, 0)

    slot = i % 2

    @pl.when(i == 0)
    def _():
        start_tile(0, 0)

    wait_tile(i, slot)

    @pl.when(i + 1 < n_tiles)
    def _():
        start_tile(i + 1, 1 - slot)

    t, d = x_ref.shape
    nc = t // SGU_CHUNK
    f = ybuf[slot, 0]
    for k in range(1, TOP_K):
        f = f + ybuf[slot, k]
    x3 = x_ref[...].reshape(nc, SGU_CHUNK, d)
    y = alpha * x3 + (1.0 + g2_ref[...]) * f.reshape(nc, SGU_CHUNK, d)
    o_ref[...] = _ln(y.reshape(t, d), lng_ref[...], lnb_ref[...])


def _moe_combine(ys, pos, x, mod3, lng, lnb, alpha):
    n, d = x.shape
    tc = COMBINE_TILE
    nc = tc // SGU_CHUNK
    n_tiles = n // tc
    grid_spec = pltpu.PrefetchScalarGridSpec(
        num_scalar_prefetch=1,
        grid=(n_tiles,),
        in_specs=[pl.BlockSpec(memory_space=pl.ANY),
                  pl.BlockSpec((tc, d), lambda i, pos: (i, 0)),
                  pl.BlockSpec((nc, 1, d), lambda i, pos: (i, 0, 5)),
                  _const_spec(lng.shape), _const_spec(lnb.shape)],
        out_specs=pl.BlockSpec((tc, d), lambda i, pos: (i, 0)),
        scratch_shapes=[pltpu.VMEM((2, TOP_K, tc, d), F32), pltpu.SemaphoreType.DMA((2,))],
    )
    return pl.pallas_call(
        functools.partial(_combine_kernel, tc=tc, n_tiles=n_tiles, alpha=alpha),
        grid_spec=grid_spec,
        out_shape=jax.ShapeDtypeStruct((n, d), F32),
        compiler_params=_cparams("arbitrary"),
        name="moe_combine",
    )(pos.reshape(-1), ys, x, mod3, lng, lnb)


def _dispatch_tables(route, real_rows, n_rows, n_exp, tm):
    n_real = real_rows.shape[0]
    sel = route[real_rows, :2 * TOP_K]
    e = sel[:, :TOP_K].astype(jnp.int32).reshape(-1)
    g = sel[:, TOP_K:].reshape(-1)
    onehot = (e[:, None] == jnp.arange(n_exp, dtype=jnp.int32)[None, :]).astype(jnp.int32)
    csum = jnp.cumsum(onehot, axis=0)
    rank = jnp.take_along_axis(csum, e[:, None], axis=1)[:, 0] - 1
    counts = csum[-1]
    padded = ((counts + tm - 1) // tm) * tm
    ends = jnp.cumsum(padded)
    dest = (ends - padded)[e] + rank
    p = -(-(TOP_K * n_real) // tm) * tm + n_exp * tm
    tok = jnp.repeat(jnp.asarray(real_rows, jnp.int32), TOP_K)
    row_token = jnp.zeros((p,), jnp.int32).at[dest].set(tok)
    row_gate = jnp.zeros((p,), F32).at[dest].set(g)
    tile_start = jnp.arange(p // tm, dtype=jnp.int32) * tm
    tile_expert = jnp.minimum(jnp.sum(tile_start[:, None] >= ends[None, :], axis=1), n_exp - 1)
    tile_valid = (tile_start < ends[-1]).astype(jnp.int32)
    pos = jnp.zeros((n_rows, TOP_K), jnp.int32).at[real_rows].set(dest.reshape(n_real, TOP_K))
    return tile_expert.astype(jnp.int32), tile_valid, row_token, row_gate, pos


def _rot_cols(w):
    half = w.shape[-1] // 2
    return jnp.concatenate([-w[..., half:], w[..., :half]], axis=-1)


def _pad_cols(w, width):
    return jnp.pad(w, ((0, 0), (0, width - w.shape[1])))


def _layout_w_in(w, q_lora, kv_lora):
    o = q_lora + kv_lora
    kpe = w[:, o:o + MLA_ROPE]
    return jnp.concatenate([w[:, :o], _pad_cols(kpe, LANE), _pad_cols(_rot_cols(kpe), LANE),
                            w[:, o + MLA_ROPE:]], axis=1).astype(BF16)


def _layout_w_uq(w):
    per = MLA_NOPE + MLA_ROPE
    main, rot = [], []
    for hd in range(MLA_HEADS):
        pe = w[:, hd * per + MLA_NOPE:(hd + 1) * per]
        main += [w[:, hd * per:hd * per + MLA_NOPE], _pad_cols(pe, LANE)]
        rot += [_pad_cols(_rot_cols(pe), LANE)]
    return jnp.concatenate(main, axis=1).astype(BF16), jnp.concatenate(rot, axis=1).astype(BF16)


def _layout_w_ukv(w):
    per = MLA_NOPE + MLA_V
    wk = jnp.concatenate([w[:, hd * per:hd * per + MLA_NOPE] for hd in range(MLA_HEADS)], axis=1)
    wv = jnp.concatenate([w[:, hd * per + MLA_NOPE:(hd + 1) * per] for hd in range(MLA_HEADS)], axis=1)
    return wk.astype(BF16), wv.astype(BF16)


def _rope_table(pos):
    half = MLA_ROPE // 2
    inv = ROPE_THETA ** (-jnp.arange(half, dtype=F32) / half)
    ang = pos.astype(F32)[:, None] * inv[None, :]
    pad = jnp.zeros((pos.shape[0], LANE - MLA_ROPE), F32)
    cos, sin = jnp.cos(ang), jnp.sin(ang)
    return jnp.concatenate([cos, cos, pad, sin, sin, pad], axis=1)


def kernel(x_prompt, x_sample, cache_ckv, cache_kpe, state_conv, c_prompt, c_sample, w_ada, b_ada, w_in, q_norm_g, kv_norm_g, w_uq, w_ukv, conv_w, sgu_ln_g, sgu_ln_b, sgu_w, sgu_b, w_o, ln_g, ln_b, ffn_w1, ffn_w3, ffn_w2, router_w, router_b, exp_w1, exp_w3, exp_w2):
    batch, seq, d = x_prompt.shape
    dec_batch, dec_seq, _ = x_sample.shape
    depth, _, past, kv_lora = cache_ckv.shape
    q_lora = q_norm_g.shape[1]
    conv_dim = conv_w.shape[2]
    sgu_dim = sgu_ln_g.shape[1]
    n_exp = router_w.shape[2]
    c = SGU_CHUNK
    assert seq % ATTN_TILE == 0 and (batch * seq) % ROW_TILE == 0
    assert CONV_WIDTH - 1 <= dec_seq <= c and (dec_batch * c) % ROW_TILE == 0
    assert sgu_dim == SGU_HEADS * LANE and (dec_batch * past) % ROW_TILE == 0
    alpha = (2 * depth) ** 0.25

    n_p = batch * seq
    n_s = dec_batch * c
    n = n_p + n_s
    n_chunks = n // c
    chunk_batch = np.concatenate([np.arange(n_p // c) // (seq // c), batch + np.arange(dec_batch)])
    real_rows = np.concatenate([np.arange(n_p),
                                (n_p + c * np.arange(dec_batch)[:, None] + np.arange(dec_seq)[None, :]).reshape(-1)])
    pos = jnp.concatenate([jnp.tile(jnp.arange(seq), batch),
                           jnp.tile(past + jnp.arange(c), dec_batch)])
    cs = _rope_table(pos)

    x = jnp.concatenate([x_prompt.reshape(n_p, d),
                         jnp.pad(x_sample, ((0, 0), (0, c - dec_seq), (0, 0))).reshape(n_s, d)], axis=0)
    n_c = batch + dec_batch
    c_all = jnp.pad(jnp.concatenate([c_prompt, c_sample], axis=0), ((0, -n_c % 8), (0, 0)))
    mod = _adaln_mod(c_all, w_ada, b_ada)

    outs = {k: [] for k in ("ckv_p", "kpe_p", "conv_p", "ckv_s", "kpe_s", "conv_s", "v_s")}
    keep = min(seq, past)
    for l in range(depth):
        mod3 = mod[l][chunk_batch].reshape(n_chunks, 1, 6 * d)
        w_in_p = _layout_w_in(w_in[l], q_lora, kv_lora)
        wq_p, wq_rot = _layout_w_uq(w_uq[l])
        wk, wv = _layout_w_ukv(w_ukv[l])
        row2 = lambda a: a.reshape(1, -1)

        q, ckv, kpe, z, bg, u, v = _in_proj(
            x, mod3, cs, w_in_p, row2(q_norm_g[l]), row2(kv_norm_g[l]), wq_p, wq_rot,
            row2(sgu_ln_g[l]), row2(sgu_ln_b[l]))

        k_new, v_new = _kv_expand(ckv, kpe, wk, wv)
        k_past, v_past = _kv_expand(cache_ckv[l].reshape(dec_batch * past, kv_lora),
                                    _pad_cols(cache_kpe[l].reshape(dec_batch * past, MLA_ROPE), LANE),
                                    wk, wv)
        attn = jnp.concatenate([
            _attn_prompt(q, k_new, v_new, batch, seq),
            _attn_sample(q, k_past, v_past, k_new, v_new, dec_batch, past, dec_seq, n_p // c)], axis=0)

        z_p = z[:n_p].reshape(batch, seq // c, c, conv_dim)[:, :, c - (CONV_WIDTH - 1):, :]
        zprev = jnp.concatenate([
            jnp.concatenate([jnp.zeros_like(z_p[:, :1]), z_p[:, :-1]], axis=1).reshape(n_p // c, CONV_WIDTH - 1, conv_dim),
            state_conv[l]], axis=0)

        moe = l % 2 == 1
        i = l // 2
        router = None
        if moe:
            w_r = _pad_cols(router_w[i], LANE)
            w_r_hi = w_r.astype(BF16)
            w_r_lo = (w_r - w_r_hi.astype(F32)).astype(BF16)
            router = (jnp.concatenate([w_r_hi, w_r_lo], axis=1),
                      _pad_cols(router_b[i].reshape(1, n_exp), LANE), n_exp)
        res = _mixer_out(attn, z, zprev, bg, u, v, conv_w[l], sgu_w[l].astype(F32), sgu_b[l].T,
                         x, mod3, row2(ln_g[l, 0]), row2(ln_b[l, 0]), w_o[l].astype(BF16), alpha,
                         router=router)
        if moe:
            x1, h2, route = res
            te, tv, row_token, row_gate, rpos = _dispatch_tables(route, real_rows, n, n_exp, ROW_TILE)
            ys = _moe_experts(h2, exp_w1[i].astype(BF16), exp_w3[i].astype(BF16),
                              exp_w2[i].astype(BF16), te, tv, row_token, row_gate)
            x = _moe_combine(ys, rpos, x1, mod3, row2(ln_g[l, 1]), row2(ln_b[l, 1]), alpha)
        else:
            x1, h2 = res
            x = _ffn_dense(h2, ffn_w1[i].astype(BF16), ffn_w3[i].astype(BF16),
                           ffn_w2[i].astype(BF16), x1, mod3, row2(ln_g[l, 1]), row2(ln_b[l, 1]), alpha)

        kpe = kpe[:, :MLA_ROPE]
        sample = lambda a: a[n_p:].reshape(dec_batch, c, a.shape[1])[:, :dec_seq]
        outs["ckv_p"].append(ckv[:n_p].reshape(batch, seq, kv_lora)[:, seq - keep:])
        outs["kpe_p"].append(kpe[:n_p].reshape(batch, seq, MLA_ROPE)[:, seq - keep:])
        outs["conv_p"].append(z[:n_p].reshape(batch, seq, conv_dim)[:, seq - (CONV_WIDTH - 1):])
        outs["ckv_s"].append(sample(ckv))
        outs["kpe_s"].append(sample(kpe))
        outs["conv_s"].append(sample(z)[:, dec_seq - (CONV_WIDTH - 1):])
        outs["v_s"].append(sample(v))

    y_p = x[:n_p].reshape(batch, seq, d)
    y_s = x[n_p:].reshape(dec_batch, c, d)[:, :dec_seq]
    st = lambda k: jnp.stack(outs[k])
    return (y_p, y_s, st("ckv_p"), st("kpe_p"), st("conv_p"), st("ckv_s"), st("kpe_s"),
            st("conv_s"), st("v_s"))
```

```python
import functools

import numpy as np
import jax
import jax.numpy as jnp
from jax import lax
from jax.experimental import pallas as pl
from jax.experimental.pallas import tpu as pltpu

F32 = jnp.float32
BF16 = jnp.bfloat16
U32 = jnp.uint32

CHUNK = 64
MLA_HEADS = 8
MLA_NOPE = 128
MLA_ROPE = 64
MLA_V = 128
ROPE_THETA = 10000.0
ATTN_SCALE = (MLA_NOPE + MLA_ROPE) ** -0.5
Q_SCALE = ATTN_SCALE * float(np.log2(np.e))
CONV_WIDTH = 3
SGU_HEADS = 4
SGU_CHUNK = 128
TOP_K = 2
LN_EPS = 1e-5
RMS_EPS = 1e-6

LANE = 128
VMEM_LIMIT = 56 * 2 ** 20

HEAD_K = 2 * LANE
HEAD_V = 2 * LANE
HEADS_PER_STEP = 2
ROW_TILE = 512
FF_TILE = 512
ATTN_TILE = 512
COMBINE_TILE = 256
GATHER_UNROLL = 8


def _cparams(*sem):
    return pltpu.CompilerParams(dimension_semantics=sem, vmem_limit_bytes=VMEM_LIMIT)


def _const_spec(shape):
    nd = len(shape)
    return pl.BlockSpec(shape, lambda *_: (0,) * nd, pipeline_mode=pl.Buffered(1))


def _rms(x, g):
    return x * lax.rsqrt(jnp.mean(x * x, axis=-1, keepdims=True) + RMS_EPS) * g


def _ln(x, g, b):
    mu = jnp.mean(x, axis=-1, keepdims=True)
    xc = x - mu
    var = jnp.mean(xc * xc, axis=-1, keepdims=True)
    return xc * lax.rsqrt(var + LN_EPS) * g + b


def _gelu(x):
    return 0.5 * x * (1.0 + lax.erf(x * (2.0 ** -0.5)))


def _silu(x):
    return x * jax.nn.sigmoid(x)


def _per_chunk(x, fn):
    t, w = x.shape
    return fn(x.reshape(t // SGU_CHUNK, SGU_CHUNK, w)).reshape(t, w)


def _mod_kernel(c_ref, w_ref, b_ref, o_ref):
    a = _silu(c_ref[...]).astype(BF16)
    o_ref[...] = jnp.dot(a, w_ref[...].astype(BF16), preferred_element_type=F32) + b_ref[...]


def _adaln_mod(c_all, w_ada, b_ada):
    depth, d, m = w_ada.shape
    r = c_all.shape[0]
    tn = m // 8
    return pl.pallas_call(
        _mod_kernel,
        grid=(depth, m // tn),
        in_specs=[pl.BlockSpec((r, d), lambda l, j: (0, 0)),
                  pl.BlockSpec((None, d, tn), lambda l, j: (l, 0, j)),
                  pl.BlockSpec((None, 1, tn), lambda l, j: (l, 0, j))],
        out_specs=pl.BlockSpec((None, r, tn), lambda l, j: (l, 0, j)),
        out_shape=jax.ShapeDtypeStruct((depth, r, m), F32),
        compiler_params=_cparams("arbitrary", "arbitrary"),
        name="adaln_mod",
    )(c_all, w_ada, b_ada.reshape(depth, 1, m))


def _in_kernel(x_ref, sh_ref, sc_ref, cs_ref, w_in_ref, qg_ref, kvg_ref, wq_ref, wqr_ref,
               lng_ref, lnb_ref, q_out, ckv_out, kpe_out, z_out, bg_out, u_out, v_out,
               *, q_lora, kv_lora, conv_dim, sgu_dim):
    t, d = x_ref.shape
    nc = t // SGU_CHUNK
    x3 = x_ref[...].reshape(nc, SGU_CHUNK, d)
    h = (x3 * (1.0 + sc_ref[...]) + sh_ref[...]).reshape(t, d)
    proj = jnp.dot(h.astype(BF16), w_in_ref[...], preferred_element_type=F32)

    o = 0
    q_lat = proj[:, o:o + q_lora]; o += q_lora
    kv_lat = proj[:, o:o + kv_lora]; o += kv_lora
    kpe_raw = proj[:, o:o + LANE]; o += LANE
    kpe_rot = proj[:, o:o + LANE]; o += LANE
    b_g = proj[:, o:o + conv_dim]; o += conv_dim
    c_g = proj[:, o:o + conv_dim]; o += conv_dim
    h_c = proj[:, o:o + conv_dim]; o += conv_dim
    u = proj[:, o:o + sgu_dim]; o += sgu_dim
    v = proj[:, o:o + sgu_dim]

    cos = cs_ref[:, :LANE]
    sin = cs_ref[:, LANE:]
    ckv_out[...] = _rms(kv_lat, kvg_ref[...])
    kpe_out[...] = kpe_raw * cos + kpe_rot * sin

    qn = _rms(q_lat, qg_ref[...]).astype(BF16)
    raw = jnp.dot(qn, wq_ref[...], preferred_element_type=F32)
    rot = jnp.dot(qn, wqr_ref[...], preferred_element_type=F32)
    for hd in range(MLA_HEADS):
        a = hd * HEAD_K
        q_out[:, a:a + LANE] = (raw[:, a:a + LANE] * Q_SCALE).astype(BF16)
        pe = raw[:, a + LANE:a + HEAD_K] * cos + rot[:, hd * LANE:(hd + 1) * LANE] * sin
        q_out[:, a + LANE:a + HEAD_K] = (pe * Q_SCALE).astype(BF16)

    bg_out[...] = b_g
    z_out[...] = c_g * h_c
    u_out[...] = _gelu(u)
    v_out[...] = _ln(_gelu(v), lng_ref[...], lnb_ref[...])


def _in_proj(x, mod3, cs, w_in_p, qg, kvg, wq_p, wq_rot, lng, lnb):
    n, d = x.shape
    q_lora, kv_lora = qg.shape[1], kvg.shape[1]
    sgu_dim = lng.shape[1]
    conv_dim = (w_in_p.shape[1] - q_lora - kv_lora - 2 * LANE - 2 * sgu_dim) // 3
    t = ROW_TILE
    nc = t // SGU_CHUNK
    row = lambda w: pl.BlockSpec((t, w), lambda i: (i, 0))
    mod = lambda k: pl.BlockSpec((nc, 1, d), lambda i: (i, 0, k))
    kern = functools.partial(_in_kernel, q_lora=q_lora, kv_lora=kv_lora, conv_dim=conv_dim,
                             sgu_dim=sgu_dim)
    widths = (MLA_HEADS * HEAD_K, kv_lora, LANE, conv_dim, conv_dim, sgu_dim, sgu_dim)
    dtypes = (BF16, F32, F32, F32, F32, F32, F32)
    return pl.pallas_call(
        kern,
        grid=(n // t,),
        in_specs=[row(d), mod(0), mod(1), row(2 * LANE), _const_spec(w_in_p.shape),
                  _const_spec(qg.shape), _const_spec(kvg.shape), _const_spec(wq_p.shape),
                  _const_spec(wq_rot.shape), _const_spec(lng.shape), _const_spec(lnb.shape)],
        out_specs=[row(w) for w in widths],
        out_shape=[jax.ShapeDtypeStruct((n, w), dt) for w, dt in zip(widths, dtypes)],
        compiler_params=_cparams("arbitrary"),
        name="in_proj",
    )(x, mod3, mod3, cs, w_in_p, qg, kvg, wq_p, wq_rot, lng, lnb)


def _kv_kernel(ckv_ref, kpe_ref, wk_ref, wv_ref, k_out, v_out):
    c = ckv_ref[...].astype(BF16)
    kn = jnp.dot(c, wk_ref[...], preferred_element_type=F32)
    vv = jnp.dot(c, wv_ref[...], preferred_element_type=F32)
    kpe = kpe_ref[...].astype(BF16)
    ones = jnp.ones((c.shape[0], HEAD_V - MLA_V), BF16)
    for hd in range(MLA_HEADS):
        a = hd * HEAD_K
        k_out[:, a:a + LANE] = kn[:, hd * MLA_NOPE:(hd + 1) * MLA_NOPE].astype(BF16)
        k_out[:, a + LANE:a + HEAD_K] = kpe
        b = hd * HEAD_V
        v_out[:, b:b + MLA_V] = vv[:, hd * MLA_V:(hd + 1) * MLA_V].astype(BF16)
        v_out[:, b + MLA_V:b + HEAD_V] = ones


def _kv_expand(ckv, kpe, wk, wv):
    n, kv_lora = ckv.shape
    t = ROW_TILE
    row = lambda w: pl.BlockSpec((t, w), lambda i: (i, 0))
    return pl.pallas_call(
        _kv_kernel,
        grid=(n // t,),
        in_specs=[row(kv_lora), row(LANE), _const_spec(wk.shape), _const_spec(wv.shape)],
        out_specs=[row(MLA_HEADS * HEAD_K), row(MLA_HEADS * HEAD_V)],
        out_shape=[jax.ShapeDtypeStruct((n, MLA_HEADS * HEAD_K), BF16),
                   jax.ShapeDtypeStruct((n, MLA_HEADS * HEAD_V), BF16)],
        compiler_params=_cparams("arbitrary"),
        name="kv_expand",
    )(ckv, kpe, wk, wv)


_NT = (((1,), (1,)), ((), ()))


def _attn_prompt_kernel(q_ref, k_ref, v_ref, o_ref, s0_ref, s1_ref, m_ref, acc_ref, *, tile):
    i = pl.program_id(2)
    heads = range(HEADS_PER_STEP)

    def scores(s_ref, j):
        off = pl.multiple_of(j * tile, tile)
        for g in heads:
            cols = slice(g * HEAD_K, (g + 1) * HEAD_K)
            s_ref[g] = lax.dot_general(q_ref[:, cols], k_ref[pl.ds(off, tile), cols], _NT,
                                       preferred_element_type=F32)

    def accumulate(s_ref, j, masked):
        off = pl.multiple_of(j * tile, tile)
        if masked:
            visible = (lax.broadcasted_iota(jnp.int32, (tile, tile), 1) // CHUNK
                       <= lax.broadcasted_iota(jnp.int32, (tile, tile), 0) // CHUNK)
        for g in heads:
            s = s_ref[g]
            if masked:
                s = jnp.where(visible, s, -jnp.inf)
            m_old = m_ref[g]
            m_new = jnp.maximum(m_old, jnp.max(s, axis=-1, keepdims=True))
            p = jnp.exp2(s - m_new).astype(BF16)
            pv = jnp.dot(p, v_ref[pl.ds(off, tile), g * HEAD_V:(g + 1) * HEAD_V],
                         preferred_element_type=F32)
            acc_ref[g] = jnp.exp2(m_old - m_new) * acc_ref[g] + pv
            m_ref[g] = m_new

    m_ref[...] = jnp.full(m_ref.shape, -jnp.inf, F32)
    acc_ref[...] = jnp.zeros(acc_ref.shape, F32)
    scores(s0_ref, 0)

    def pair(jj, c):
        scores(s1_ref, 2 * jj + 1)
        accumulate(s0_ref, 2 * jj, False)
        scores(s0_ref, 2 * jj + 2)
        accumulate(s1_ref, 2 * jj + 1, False)
        return c

    lax.fori_loop(0, i // 2, pair, 0)

    @pl.when(i % 2 == 0)
    def _():
        accumulate(s0_ref, i, True)

    @pl.when(i % 2 == 1)
    def _():
        scores(s1_ref, i)
        accumulate(s0_ref, i - 1, False)
        accumulate(s1_ref, i, True)

    for g in heads:
        acc = acc_ref[g]
        o_ref[:, g * MLA_V:(g + 1) * MLA_V] = (acc[:, :MLA_V] / acc[:, MLA_V:]).astype(o_ref.dtype)


def _attn_prompt(q, k, v, batch, seq):
    t = ATTN_TILE
    nq = seq // t
    g = HEADS_PER_STEP
    return pl.pallas_call(
        functools.partial(_attn_prompt_kernel, tile=t),
        grid=(batch, MLA_HEADS // g, nq),
        in_specs=[pl.BlockSpec((t, g * HEAD_K), lambda b, h, i: (b * nq + i, h)),
                  pl.BlockSpec((seq, g * HEAD_K), lambda b, h, i: (b, h)),
                  pl.BlockSpec((seq, g * HEAD_V), lambda b, h, i: (b, h))],
        out_specs=pl.BlockSpec((t, g * MLA_V), lambda b, h, i: (b * nq + i, h)),
        out_shape=jax.ShapeDtypeStruct((batch * seq, MLA_HEADS * MLA_V), BF16),
        scratch_shapes=[pltpu.VMEM((g, t, t), F32), pltpu.VMEM((g, t, t), F32),
                        pltpu.VMEM((g, t, 1), F32), pltpu.VMEM((g, t, HEAD_V), F32)],
        compiler_params=_cparams("arbitrary", "arbitrary", "arbitrary"),
        name="attn_prompt",
    )(q, k, v)


def _attn_sample_kernel(q_ref, kp_ref, vp_ref, kn_ref, vn_ref, o_ref, *, n_new):
    q = q_ref[...]
    s1 = lax.dot_general(q, kp_ref[...], _NT, preferred_element_type=F32)
    s2 = lax.dot_general(q, kn_ref[...], _NT, preferred_element_type=F32)
    s2 = jnp.where(lax.broadcasted_iota(jnp.int32, s2.shape, 1) < n_new, s2, -jnp.inf)
    m = jnp.maximum(jnp.max(s1, axis=-1, keepdims=True), jnp.max(s2, axis=-1, keepdims=True))
    acc = jnp.dot(jnp.exp2(s1 - m).astype(BF16), vp_ref[...], preferred_element_type=F32)
    acc = acc + jnp.dot(jnp.exp2(s2 - m).astype(BF16), vn_ref[...], preferred_element_type=F32)
    o_ref[...] = (acc[:, :MLA_V] / acc[:, MLA_V:]).astype(o_ref.dtype)


def _attn_sample(q, k_past, v_past, k, v, dec_batch, past, n_new, first_chunk):
    c = SGU_CHUNK
    new = lambda w: pl.BlockSpec((c, w), lambda b, h: (first_chunk + b, h))
    old = lambda w: pl.BlockSpec((past, w), lambda b, h: (b, h))
    return pl.pallas_call(
        functools.partial(_attn_sample_kernel, n_new=n_new),
        grid=(dec_batch, MLA_HEADS),
        in_specs=[new(HEAD_K), old(HEAD_K), old(HEAD_V), new(HEAD_K), new(HEAD_V)],
        out_specs=pl.BlockSpec((c, MLA_V), lambda b, h: (b, h)),
        out_shape=jax.ShapeDtypeStruct((dec_batch * c, MLA_HEADS * MLA_V), BF16),
        compiler_params=_cparams("arbitrary", "arbitrary"),
        name="attn_sample",
    )(q, k_past, v_past, k, v)


def _route(h, wr_ref, br_ref, n_exp):
    hi = h.astype(BF16)
    lo = (h - hi.astype(F32)).astype(BF16)
    a = jnp.dot(hi, wr_ref[...], preferred_element_type=F32)
    b = jnp.dot(lo, wr_ref[:, :LANE], preferred_element_type=F32)
    logits = a[:, :LANE] + (a[:, LANE:] + b) + br_ref[...]
    lane = lax.broadcasted_iota(jnp.int32, logits.shape, 1)
    lg = jnp.where(lane < n_exp, logits, -jnp.inf)
    v1 = jnp.max(lg, axis=-1, keepdims=True)
    i1 = jnp.min(jnp.where(lg == v1, lane, LANE), axis=-1, keepdims=True)
    lg2 = jnp.where(lane == i1, -jnp.inf, lg)
    v2 = jnp.max(lg2, axis=-1, keepdims=True)
    i2 = jnp.min(jnp.where(lg2 == v2, lane, LANE), axis=-1, keepdims=True)
    e = jnp.exp(v2 - v1)
    g1 = 1.0 / (1.0 + e)
    g2 = e / (1.0 + e)
    out = jnp.where(lane == 0, i1.astype(F32), 0.0)
    out = jnp.where(lane == 1, i2.astype(F32), out)
    out = jnp.where(lane == 2, g1, out)
    return jnp.where(lane == 3, g2, out)


def _pack_bf16_pairs(h):
    w = h.shape[1] // 2
    hi = pltpu.bitcast(h[:, :w].astype(BF16).astype(F32), U32)
    lo = pltpu.bitcast(h[:, w:].astype(BF16).astype(F32), U32)
    return (hi & jnp.uint32(0xFFFF0000)) | (lo >> 16)


def _unpack_bf16_pairs(w):
    hi = pltpu.bitcast(w & jnp.uint32(0xFFFF0000), F32).astype(BF16)
    lo = pltpu.bitcast(w << 16, F32).astype(BF16)
    return hi, lo


def _out_kernel(*refs, alpha, n_exp, routed):
    if routed:
        (attn_ref, z_ref, zprev_ref, bg_ref, u_ref, v_ref, cw_ref, sw_ref, sb_ref, x_ref,
         g1_ref, sh2_ref, sc2_ref, lng_ref, lnb_ref, wo_ref, wr_ref, br_ref,
         x_out, h_out, route_out, mix_ref) = refs
    else:
        (attn_ref, z_ref, zprev_ref, bg_ref, u_ref, v_ref, cw_ref, sw_ref, sb_ref, x_ref,
         g1_ref, sh2_ref, sc2_ref, lng_ref, lnb_ref, wo_ref,
         x_out, h_out, mix_ref) = refs
    t, d = x_ref.shape
    nc = t // SGU_CHUNK
    n_attn = attn_ref.shape[1]
    conv_dim = z_ref.shape[1]

    z = z_ref[...]
    zp = zprev_ref[...]
    pos = lax.broadcasted_iota(jnp.int32, (nc, SGU_CHUNK, conv_dim), 1)
    r1 = pltpu.roll(z, 1, 0).reshape(nc, SGU_CHUNK, conv_dim)
    r2 = pltpu.roll(z, 2, 0).reshape(nc, SGU_CHUNK, conv_dim)
    r1 = jnp.where(pos == 0, zp[:, 1:2, :], r1)
    r2 = jnp.where(pos == 0, zp[:, 0:1, :], jnp.where(pos == 1, zp[:, 1:2, :], r2))
    cw = cw_ref[...]
    y = cw[0:1, :] * r2 + cw[1:2, :] * r1 + cw[2:3, :] * z.reshape(nc, SGU_CHUNK, conv_dim)
    conv = bg_ref[...] * y.reshape(t, conv_dim)
    mix_ref[:, :n_attn] = attn_ref[...]
    mix_ref[:, n_attn:n_attn + conv_dim] = conv.astype(BF16)

    tri = (lax.broadcasted_iota(jnp.int32, (SGU_CHUNK, SGU_CHUNK), 0)
           >= lax.broadcasted_iota(jnp.int32, (SGU_CHUNK, SGU_CHUNK), 1))
    hw = v_ref.shape[1] // SGU_HEADS
    for hd in range(SGU_HEADS):
        w_m = jnp.where(tri, sw_ref[hd], 0.0).astype(BF16)
        bias = sb_ref[:, hd:hd + 1]
        for c in range(nc):
            rows = slice(c * SGU_CHUNK, (c + 1) * SGU_CHUNK)
            cols = slice(hd * hw, (hd + 1) * hw)
            mixed = jnp.dot(w_m, v_ref[rows, cols].astype(BF16), preferred_element_type=F32) + bias
            mix_ref[rows, n_attn + conv_dim + hd * hw:n_attn + conv_dim + (hd + 1) * hw] = (
                u_ref[rows, cols] * mixed).astype(BF16)

    o = jnp.dot(mix_ref[...], wo_ref[...], preferred_element_type=F32)
    x3 = x_ref[...].reshape(nc, SGU_CHUNK, d)
    yres = (alpha * x3 + (1.0 + g1_ref[...]) * o.reshape(nc, SGU_CHUNK, d)).reshape(t, d)
    x1 = _ln(yres, lng_ref[...], lnb_ref[...])
    x_out[...] = x1
    h2 = (x1.reshape(nc, SGU_CHUNK, d) * (1.0 + sc2_ref[...]) + sh2_ref[...]).reshape(t, d)
    if routed:
        _store_token_tiles(h_out, _pack_bf16_pairs(h2))
        route_out[...] = _route(h2, wr_ref, br_ref, n_exp)
    else:
        h_out[...] = h2.astype(BF16)


def _mixer_out(attn, z, zprev, bg, u, v, conv_w, sgu_w, sgu_bt, x, mod3, lng, lnb, w_o,
               alpha, router=None):
    n, d = x.shape
    t = ROW_TILE
    nc = t // SGU_CHUNK
    routed = router is not None
    row = lambda w: pl.BlockSpec((t, w), lambda i: (i, 0))
    mod = lambda k: pl.BlockSpec((nc, 1, d), lambda i: (i, 0, k))
    in_specs = [row(attn.shape[1]), row(z.shape[1]),
                pl.BlockSpec((nc,) + zprev.shape[1:], lambda i: (i, 0, 0)),
                row(bg.shape[1]), row(u.shape[1]), row(v.shape[1]),
                _const_spec(conv_w.shape), _const_spec(sgu_w.shape), _const_spec(sgu_bt.shape),
                row(d), mod(2), mod(3), mod(4), _const_spec(lng.shape), _const_spec(lnb.shape),
                _const_spec(w_o.shape)]
    args = [attn, z, zprev, bg, u, v, conv_w, sgu_w, sgu_bt, x, mod3, mod3, mod3, lng, lnb, w_o]
    out_specs = [row(d)]
    out_shape = [jax.ShapeDtypeStruct((n, d), F32)]
    n_exp = 0
    if routed:
        w_r, b_r, n_exp = router
        in_specs += [_const_spec(w_r.shape), _const_spec(b_r.shape)]
        args += [w_r, b_r]
        hw = d // 2 // LANE
        out_specs += [pl.BlockSpec((t * hw, LANE), lambda i: (i, 0)), row(LANE)]
        out_shape += [jax.ShapeDtypeStruct((n * hw, LANE), U32), jax.ShapeDtypeStruct((n, LANE), F32)]
    else:
        out_specs += [row(d)]
        out_shape += [jax.ShapeDtypeStruct((n, d), BF16)]
    return pl.pallas_call(
        functools.partial(_out_kernel, alpha=alpha, n_exp=n_exp, routed=routed),
        grid=(n // t,),
        in_specs=in_specs,
        out_specs=out_specs,
        out_shape=out_shape,
        scratch_shapes=[pltpu.VMEM((t, d), BF16)],
        compiler_params=_cparams("arbitrary"),
        name="mixer_out_routed" if routed else "mixer_out",
    )(*args)


def _ffn_kernel(h_ref, w1_ref, w3_ref, w2_ref, x_ref, g2_ref, lng_ref, lnb_ref, o_ref, acc_ref,
                *, alpha):
    j = pl.program_id(1)

    @pl.when(j == 0)
    def _():
        acc_ref[...] = jnp.zeros_like(acc_ref)

    h = h_ref[...]
    a = jnp.dot(h, w1_ref[...], preferred_element_type=F32)
    b = jnp.dot(h, w3_ref[...], preferred_element_type=F32)
    acc_ref[...] += jnp.dot((_silu(a) * b).astype(BF16), w2_ref[...], preferred_element_type=F32)

    @pl.when(j == pl.num_programs(1) - 1)
    def _():
        t, d = x_ref.shape
        nc = t // SGU_CHUNK
        x3 = x_ref[...].reshape(nc, SGU_CHUNK, d)
        y = alpha * x3 + (1.0 + g2_ref[...]) * acc_ref[...].reshape(nc, SGU_CHUNK, d)
        o_ref[...] = _ln(y.reshape(t, d), lng_ref[...], lnb_ref[...])


def _ffn_dense(h, w1, w3, w2, x, mod3, lng, lnb, alpha):
    n, d = x.shape
    nj, _, tf = w1.shape
    t = ROW_TILE
    nc = t // SGU_CHUNK
    return pl.pallas_call(
        functools.partial(_ffn_kernel, alpha=alpha),
        grid=(n // t, nj),
        in_specs=[pl.BlockSpec((t, d), lambda i, j: (i, 0)),
                  pl.BlockSpec((None, d, tf), lambda i, j: (j, 0, 0)),
                  pl.BlockSpec((None, d, tf), lambda i, j: (j, 0, 0)),
                  pl.BlockSpec((tf, d), lambda i, j: (j, 0)),
                  pl.BlockSpec((t, d), lambda i, j: (i, 0)),
                  pl.BlockSpec((nc, 1, d), lambda i, j: (i, 0, 5)),
                  _const_spec(lng.shape), _const_spec(lnb.shape)],
        out_specs=pl.BlockSpec((t, d), lambda i, j: (i, 0)),
        out_shape=jax.ShapeDtypeStruct((n, d), F32),
        scratch_shapes=[pltpu.VMEM((t, d), F32)],
        compiler_params=_cparams("arbitrary", "arbitrary"),
        name="ffn_dense",
    )(h, w1, w3, w2, x, mod3, lng, lnb)


def _moe_kernel(te_ref, tv_ref, rt_ref, h_hbm, w1_ref, w3_ref, w2_ref, o_ref,
                hbuf, hb, acc_ref, sem, *, tm, n_tiles):
    r = pl.program_id(0)
    j = pl.program_id(1)
    half = hb.shape[1] // 2
    wt = half // LANE

    def row_copy(tile, slot, i):
        tok = rt_ref[tile * tm + i]
        return pltpu.make_async_copy(h_hbm.at[pl.ds(pl.multiple_of(tok * wt, wt), wt), :],
                                     hbuf.at[slot, pl.ds(pl.multiple_of(i * wt, wt), wt), :],
                                     sem.at[slot])

    def start_tile(tile, slot):
        def body(i, c):
            row_copy(tile, slot, i).start()
            return c
        lax.fori_loop(0, tm, body, 0, unroll=GATHER_UNROLL)

    def wait_tile(tile, slot):
        def body(i, c):
            row_copy(tile, slot, i).wait()
            return c
        lax.fori_loop(0, tm, body, 0, unroll=GATHER_UNROLL)

    slot = r % 2

    @pl.when(j == 0)
    def _():
        @pl.when(r == 0)
        def _():
            start_tile(0, 0)

        wait_tile(r, slot)

        @pl.when(r + 1 < n_tiles)
        def _():
            start_tile(r + 1, 1 - slot)

        for c in range(wt):
            hi, lo = _unpack_bf16_pairs(hbuf[slot, pl.ds(c, tm, stride=wt), :])
            hb[:, c * LANE:(c + 1) * LANE] = hi
            hb[:, half + c * LANE:half + (c + 1) * LANE] = lo
        acc_ref[...] = jnp.zeros_like(acc_ref)

    @pl.when(tv_ref[r] == 1)
    def _():
        h = hb[...]
        a = jnp.dot(h, w1_ref[...], preferred_element_type=F32)
        b = jnp.dot(h, w3_ref[...], preferred_element_type=F32)
        acc_ref[...] += jnp.dot((_silu(a) * b).astype(BF16), w2_ref[...],
                                preferred_element_type=F32)

    @pl.when(j == pl.num_programs(1) - 1)
    def _():
        _store_token_tiles(o_ref, acc_ref[...])


def _store_token_tiles(o_ref, x):
    t, w = x.shape
    wt = w // LANE
    for c in range(wt):
        o_ref[pl.ds(c, t, stride=wt), :] = x[:, c * LANE:(c + 1) * LANE]


def _moe_experts(h_packed, w1, w3, w2, tile_expert, tile_valid, row_token):
    n_exp, nj, d, tf = w1.shape
    p = row_token.shape[0]
    tm = ROW_TILE
    n_tiles = p // tm
    hw, ow = d // 2 // LANE, d // LANE
    col = lambda j, tv, r: j * tv[r] + (nj - 1) * (1 - tv[r])
    grid_spec = pltpu.PrefetchScalarGridSpec(
        num_scalar_prefetch=3,
        grid=(n_tiles, nj),
        in_specs=[pl.BlockSpec(memory_space=pl.ANY),
                  pl.BlockSpec((None, None, d, tf), lambda r, j, te, tv, rt: (te[r], col(j, tv, r), 0, 0)),
                  pl.BlockSpec((None, None, d, tf), lambda r, j, te, tv, rt: (te[r], col(j, tv, r), 0, 0)),
                  pl.BlockSpec((None, tf, d), lambda r, j, te, tv, rt: (te[r], col(j, tv, r), 0))],
        out_specs=pl.BlockSpec((tm * ow, LANE), lambda r, j, te, tv, rt: (r, 0)),
        scratch_shapes=[pltpu.VMEM((2, tm * hw, LANE), U32), pltpu.VMEM((tm, d), BF16),
                        pltpu.VMEM((tm, d), F32), pltpu.SemaphoreType.DMA((2,))],
    )
    return pl.pallas_call(
        functools.partial(_moe_kernel, tm=tm, n_tiles=n_tiles),
        grid_spec=grid_spec,
        out_shape=jax.ShapeDtypeStruct((p * ow, LANE), F32),
        compiler_params=_cparams("arbitrary", "arbitrary"),
        name="moe_experts",
    )(tile_expert, tile_valid, row_token, h_packed, w1, w3, w2)


def _combine_kernel(pos_ref, ys_hbm, route_ref, x_ref, g2_ref, lng_ref, lnb_ref, o_ref,
                    ybuf, fbuf, sem, *, tc, n_tiles, alpha):
    i = pl.program_id(0)
    t, d = x_ref.shape
    wt = d // LANE

    def row_copy(tile, slot, k, r):
        p = pos_ref[(tile * tc + r) * TOP_K + k]
        return pltpu.make_async_copy(ys_hbm.at[pl.ds(pl.multiple_of(p * wt, wt), wt), :],
                                     ybuf.at[slot, k, pl.ds(pl.multiple_of(r * wt, wt), wt), :],
                                     sem.at[slot])

    def start_tile(tile, slot):
        def body(r, c):
            for k in range(TOP_K):
                row_copy(tile, slot, k, r).start()
            return c
        lax.fori_loop(0, tc, body, 0, unroll=GATHER_UNROLL)

    def wait_tile(tile, slot):
        def body(r, c):
            for k in range(TOP_K):
                row_copy(tile, slot, k, r).wait()
            return c
        lax.fori_loop(0, tc, body, 0, unroll=GATHER_UNROLL)

    slot = i % 2

    @pl.when(i == 0)
    def _():
        start_tile(0, 0)

    wait_tile(i, slot)

    @pl.when(i + 1 < n_tiles)
    def _():
        start_tile(i + 1, 1 - slot)

    nc = t // SGU_CHUNK
    gates = [route_ref[:, TOP_K + k:TOP_K + k + 1] for k in range(TOP_K)]
    for c in range(wt):
        f = gates[0] * ybuf[slot, 0, pl.ds(c, t, stride=wt), :]
        for k in range(1, TOP_K):
            f = f + gates[k] * ybuf[slot, k, pl.ds(c, t, stride=wt), :]
        fbuf[:, c * LANE:(c + 1) * LANE] = f
    x3 = x_ref[...].reshape(nc, SGU_CHUNK, d)
    y = alpha * x3 + (1.0 + g2_ref[...]) * fbuf[...].reshape(nc, SGU_CHUNK, d)
    o_ref[...] = _ln(y.reshape(t, d), lng_ref[...], lnb_ref[...])


def _moe_combine(ys, pos, route, x, mod3, lng, lnb, alpha):
    n, d = x.shape
    tc = COMBINE_TILE
    nc = tc // SGU_CHUNK
    n_tiles = n // tc
    wt = d // LANE
    grid_spec = pltpu.PrefetchScalarGridSpec(
        num_scalar_prefetch=1,
        grid=(n_tiles,),
        in_specs=[pl.BlockSpec(memory_space=pl.ANY),
                  pl.BlockSpec((tc, LANE), lambda i, pos: (i, 0)),
                  pl.BlockSpec((tc, d), lambda i, pos: (i, 0)),
                  pl.BlockSpec((nc, 1, d), lambda i, pos: (i, 0, 5)),
                  _const_spec(lng.shape), _const_spec(lnb.shape)],
        out_specs=pl.BlockSpec((tc, d), lambda i, pos: (i, 0)),
        scratch_shapes=[pltpu.VMEM((2, TOP_K, tc * wt, LANE), F32), pltpu.VMEM((tc, d), F32),
                        pltpu.SemaphoreType.DMA((2,))],
    )
    return pl.pallas_call(
        functools.partial(_combine_kernel, tc=tc, n_tiles=n_tiles, alpha=alpha),
        grid_spec=grid_spec,
        out_shape=jax.ShapeDtypeStruct((n, d), F32),
        compiler_params=_cparams("arbitrary"),
        name="moe_combine",
    )(pos.reshape(-1), ys, route, x, mod3, lng, lnb)


def _dispatch_tables(route, real, n_exp, tm):
    n = real.shape[0]
    real2 = jnp.asarray(np.repeat(real, TOP_K))
    e = route[:, :TOP_K].astype(jnp.int32).reshape(-1)
    onehot = ((e[:, None] == jnp.arange(n_exp, dtype=jnp.int32)[None, :]) & real2[:, None]).astype(jnp.int32)
    csum = jnp.cumsum(onehot, axis=0)
    rank = jnp.take_along_axis(csum, e[:, None], axis=1)[:, 0] - 1
    counts = csum[-1]
    padded = ((counts + tm - 1) // tm) * tm
    ends = jnp.cumsum(padded)
    dest = (ends - padded)[e] + rank
    p = -(-(TOP_K * int(real.sum())) // tm) * tm + n_exp * tm
    tok = jnp.repeat(jnp.arange(n, dtype=jnp.int32), TOP_K)
    row_token = jnp.zeros((p,), jnp.int32).at[jnp.where(real2, dest, p)].set(tok, mode="drop")
    tile_start = jnp.arange(p // tm, dtype=jnp.int32) * tm
    tile_expert = jnp.minimum(jnp.sum(tile_start[:, None] >= ends[None, :], axis=1), n_exp - 1)
    tile_valid = (tile_start < ends[-1]).astype(jnp.int32)
    pos = jnp.where(real2, dest, 0)
    return tile_expert.astype(jnp.int32), tile_valid, row_token, pos


def _rot_cols(w):
    half = w.shape[-1] // 2
    return jnp.concatenate([-w[..., half:], w[..., :half]], axis=-1)


def _pad_cols(w, width):
    return jnp.pad(w, ((0, 0), (0, width - w.shape[1])))


def _layout_w_in(w, q_lora, kv_lora):
    o = q_lora + kv_lora
    kpe = w[:, o:o + MLA_ROPE]
    return jnp.concatenate([w[:, :o], _pad_cols(kpe, LANE), _pad_cols(_rot_cols(kpe), LANE),
                            w[:, o + MLA_ROPE:]], axis=1).astype(BF16)


def _layout_w_uq(w):
    per = MLA_NOPE + MLA_ROPE
    main, rot = [], []
    for hd in range(MLA_HEADS):
        pe = w[:, hd * per + MLA_NOPE:(hd + 1) * per]
        main += [w[:, hd * per:hd * per + MLA_NOPE], _pad_cols(pe, LANE)]
        rot += [_pad_cols(_rot_cols(pe), LANE)]
    return jnp.concatenate(main, axis=1).astype(BF16), jnp.concatenate(rot, axis=1).astype(BF16)


def _layout_w_ukv(w):
    per = MLA_NOPE + MLA_V
    wk = jnp.concatenate([w[:, hd * per:hd * per + MLA_NOPE] for hd in range(MLA_HEADS)], axis=1)
    wv = jnp.concatenate([w[:, hd * per + MLA_NOPE:(hd + 1) * per] for hd in range(MLA_HEADS)], axis=1)
    return wk.astype(BF16), wv.astype(BF16)


def _tile_cols(w):
    *lead, d, f = w.shape
    w = w.astype(BF16).reshape(*lead, d, f // FF_TILE, FF_TILE)
    return jnp.swapaxes(w, -3, -2)


def _rope_table(pos):
    half = MLA_ROPE // 2
    inv = ROPE_THETA ** (-jnp.arange(half, dtype=F32) / half)
    ang = pos.astype(F32)[:, None] * inv[None, :]
    pad = jnp.zeros((pos.shape[0], LANE - MLA_ROPE), F32)
    cos, sin = jnp.cos(ang), jnp.sin(ang)
    return jnp.concatenate([cos, cos, pad, sin, sin, pad], axis=1)


def kernel(x_prompt, x_sample, cache_ckv, cache_kpe, state_conv, c_prompt, c_sample, w_ada, b_ada, w_in, q_norm_g, kv_norm_g, w_uq, w_ukv, conv_w, sgu_ln_g, sgu_ln_b, sgu_w, sgu_b, w_o, ln_g, ln_b, ffn_w1, ffn_w3, ffn_w2, router_w, router_b, exp_w1, exp_w3, exp_w2):
    batch, seq, d = x_prompt.shape
    dec_batch, dec_seq, _ = x_sample.shape
    depth, _, past, kv_lora = cache_ckv.shape
    q_lora = q_norm_g.shape[1]
    conv_dim = conv_w.shape[2]
    sgu_dim = sgu_ln_g.shape[1]
    n_exp = router_w.shape[2]
    c = SGU_CHUNK
    assert seq % ATTN_TILE == 0 and (batch * seq) % ROW_TILE == 0
    assert CONV_WIDTH - 1 <= dec_seq <= c and (dec_batch * c) % ROW_TILE == 0
    assert sgu_dim == SGU_HEADS * LANE and (dec_batch * past) % ROW_TILE == 0
    alpha = (2 * depth) ** 0.25

    n_p = batch * seq
    n_s = dec_batch * c
    n = n_p + n_s
    n_chunks = n // c
    chunk_batch = np.concatenate([np.arange(n_p // c) // (seq // c), batch + np.arange(dec_batch)])
    real = np.concatenate([np.ones(n_p, bool), np.tile(np.arange(c) < dec_seq, dec_batch)])
    pos = jnp.concatenate([jnp.tile(jnp.arange(seq), batch),
                           jnp.tile(past + jnp.arange(c), dec_batch)])
    cs = _rope_table(pos)

    x = jnp.concatenate([x_prompt.reshape(n_p, d),
                         jnp.pad(x_sample, ((0, 0), (0, c - dec_seq), (0, 0))).reshape(n_s, d)], axis=0)
    n_c = batch + dec_batch
    c_all = jnp.pad(jnp.concatenate([c_prompt, c_sample], axis=0), ((0, -n_c % 8), (0, 0)))
    mod = _adaln_mod(c_all, w_ada, b_ada)

    outs = {k: [] for k in ("ckv_p", "kpe_p", "conv_p", "ckv_s", "kpe_s", "conv_s", "v_s")}
    keep = min(seq, past)
    for l in range(depth):
        mod3 = mod[l][chunk_batch].reshape(n_chunks, 1, 6 * d)
        w_in_p = _layout_w_in(w_in[l], q_lora, kv_lora)
        wq_p, wq_rot = _layout_w_uq(w_uq[l])
        wk, wv = _layout_w_ukv(w_ukv[l])
        row2 = lambda a: a.reshape(1, -1)

        q, ckv, kpe, z, bg, u, v = _in_proj(
            x, mod3, cs, w_in_p, row2(q_norm_g[l]), row2(kv_norm_g[l]), wq_p, wq_rot,
            row2(sgu_ln_g[l]), row2(sgu_ln_b[l]))

        k_new, v_new = _kv_expand(ckv, kpe, wk, wv)
        k_past, v_past = _kv_expand(cache_ckv[l].reshape(dec_batch * past, kv_lora),
                                    _pad_cols(cache_kpe[l].reshape(dec_batch * past, MLA_ROPE), LANE),
                                    wk, wv)
        attn = jnp.concatenate([
            _attn_prompt(q, k_new, v_new, batch, seq),
            _attn_sample(q, k_past, v_past, k_new, v_new, dec_batch, past, dec_seq, n_p // c)], axis=0)

        z_p = z[:n_p].reshape(batch, seq // c, c, conv_dim)[:, :, c - (CONV_WIDTH - 1):, :]
        zprev = jnp.concatenate([
            jnp.concatenate([jnp.zeros_like(z_p[:, :1]), z_p[:, :-1]], axis=1).reshape(n_p // c, CONV_WIDTH - 1, conv_dim),
            state_conv[l]], axis=0)

        moe = l % 2 == 1
        i = l // 2
        router = None
        if moe:
            w_r = _pad_cols(router_w[i], LANE)
            w_r_hi = w_r.astype(BF16)
            w_r_lo = (w_r - w_r_hi.astype(F32)).astype(BF16)
            router = (jnp.concatenate([w_r_hi, w_r_lo], axis=1),
                      _pad_cols(router_b[i].reshape(1, n_exp), LANE), n_exp)
        res = _mixer_out(attn, z, zprev, bg, u, v, conv_w[l], sgu_w[l].astype(F32), sgu_b[l].T,
                         x, mod3, row2(ln_g[l, 0]), row2(ln_b[l, 0]), w_o[l].astype(BF16), alpha,
                         router=router)
        if moe:
            x1, h2, route = res
            te, tv, row_token, rpos = _dispatch_tables(route, real, n_exp, ROW_TILE)
            ys = _moe_experts(h2, _tile_cols(exp_w1[i]), _tile_cols(exp_w3[i]),
                              exp_w2[i].astype(BF16), te, tv, row_token)
            x = _moe_combine(ys, rpos, route, x1, mod3, row2(ln_g[l, 1]), row2(ln_b[l, 1]), alpha)
        else:
            x1, h2 = res
            x = _ffn_dense(h2, _tile_cols(ffn_w1[i]), _tile_cols(ffn_w3[i]),
                           ffn_w2[i].astype(BF16), x1, mod3, row2(ln_g[l, 1]), row2(ln_b[l, 1]), alpha)

        kpe = kpe[:, :MLA_ROPE]
        sample = lambda a: a[n_p:].reshape(dec_batch, c, a.shape[1])[:, :dec_seq]
        outs["ckv_p"].append(ckv[:n_p].reshape(batch, seq, kv_lora)[:, seq - keep:])
        outs["kpe_p"].append(kpe[:n_p].reshape(batch, seq, MLA_ROPE)[:, seq - keep:])
        outs["conv_p"].append(z[:n_p].reshape(batch, seq, conv_dim)[:, seq - (CONV_WIDTH - 1):])
        outs["ckv_s"].append(sample(ckv))
        outs["kpe_s"].append(sample(kpe))
        outs["conv_s"].append(sample(z)[:, dec_seq - (CONV_WIDTH - 1):])
        outs["v_s"].append(sample(v))

    y_p = x[:n_p].reshape(batch, seq, d)
    y_s = x[n_p:].reshape(dec_batch, c, d)[:, :dec_seq]
    st = lambda k: jnp.stack(outs[k])
    return (y_p, y_s, st("ckv_p"), st("kpe_p"), st("conv_p"), st("ckv_s"), st("kpe_s"),
            st("conv_s"), st("v_s"))
```

```python
import functools

import numpy as np
import jax
import jax.numpy as jnp
from jax import lax
from jax.experimental import pallas as pl
from jax.experimental.pallas import tpu as pltpu

F32 = jnp.float32
BF16 = jnp.bfloat16
U32 = jnp.uint32

CHUNK = 64
MLA_HEADS = 8
MLA_NOPE = 128
MLA_ROPE = 64
MLA_V = 128
ROPE_THETA = 10000.0
ATTN_SCALE = (MLA_NOPE + MLA_ROPE) ** -0.5
Q_SCALE = ATTN_SCALE * float(np.log2(np.e))
CONV_WIDTH = 3
SGU_HEADS = 4
SGU_CHUNK = 128
TOP_K = 2
LN_EPS = 1e-5
RMS_EPS = 1e-6

LANE = 128
VMEM_LIMIT = 56 * 2 ** 20

HEAD_K = 2 * LANE
HEAD_V = 2 * LANE
HEADS_PER_STEP = 2
ROW_TILE = 512
FF_TILE = 512
MOE_FF_TILE = 1024
ATTN_TILE = 512
COMBINE_TILE = 256
GATHER_UNROLL = 8


def _cparams(*sem):
    return pltpu.CompilerParams(dimension_semantics=sem, vmem_limit_bytes=VMEM_LIMIT)


def _const_spec(shape):
    nd = len(shape)
    return pl.BlockSpec(shape, lambda *_: (0,) * nd, pipeline_mode=pl.Buffered(1))


def _rms(x, g):
    return x * lax.rsqrt(jnp.mean(x * x, axis=-1, keepdims=True) + RMS_EPS) * g


def _ln(x, g, b):
    mu = jnp.mean(x, axis=-1, keepdims=True)
    xc = x - mu
    var = jnp.mean(xc * xc, axis=-1, keepdims=True)
    return xc * lax.rsqrt(var + LN_EPS) * g + b


def _gelu(x):
    return 0.5 * x * (1.0 + lax.erf(x * (2.0 ** -0.5)))


def _silu(x):
    return x * jax.nn.sigmoid(x)


def _per_chunk(x, fn):
    t, w = x.shape
    return fn(x.reshape(t // SGU_CHUNK, SGU_CHUNK, w)).reshape(t, w)


def _mod_kernel(c_ref, w_ref, b_ref, o_ref):
    a = _silu(c_ref[...]).astype(BF16)
    o_ref[...] = jnp.dot(a, w_ref[...].astype(BF16), preferred_element_type=F32) + b_ref[...]


def _adaln_mod(c_all, w_ada, b_ada):
    depth, d, m = w_ada.shape
    r = c_all.shape[0]
    tn = m // 8
    return pl.pallas_call(
        _mod_kernel,
        grid=(depth, m // tn),
        in_specs=[pl.BlockSpec((r, d), lambda l, j: (0, 0)),
                  pl.BlockSpec((None, d, tn), lambda l, j: (l, 0, j)),
                  pl.BlockSpec((None, 1, tn), lambda l, j: (l, 0, j))],
        out_specs=pl.BlockSpec((None, r, tn), lambda l, j: (l, 0, j)),
        out_shape=jax.ShapeDtypeStruct((depth, r, m), F32),
        compiler_params=_cparams("arbitrary", "arbitrary"),
        name="adaln_mod",
    )(c_all, w_ada, b_ada.reshape(depth, 1, m))


def _part_specs(parts, t):
    specs, starts, start = [], [], 0
    for a in parts:
        tiles = a.shape[0] // t
        specs.append(pl.BlockSpec((t, a.shape[1]),
                                  lambda i, s=start, m=tiles - 1: (jnp.clip(i - s, 0, m), 0)))
        starts.append(start)
        start += tiles
    return specs, tuple(starts)


def _read_parts(refs, starts):
    i = pl.program_id(0)
    x = refs[0][...]
    for ref, s in zip(refs[1:], starts[1:]):
        x = jnp.where(i >= s, ref[...], x)
    return x


def _in_kernel(*refs, x_starts, q_lora, kv_lora, conv_dim, sgu_dim):
    nx = len(x_starts)
    (sh_ref, sc_ref, cs_ref, w_in_ref, qg_ref, kvg_ref, wq_ref, wqr_ref, lng_ref, lnb_ref,
     q_out, ckv_out, kpe_out, z_out, bg_out, u_out, v_out) = refs[nx:]
    t, d = refs[0].shape
    nc = t // SGU_CHUNK
    x3 = _read_parts(refs[:nx], x_starts).reshape(nc, SGU_CHUNK, d)
    h = (x3 * (1.0 + sc_ref[...]) + sh_ref[...]).reshape(t, d)
    proj = jnp.dot(h.astype(BF16), w_in_ref[...], preferred_element_type=F32)

    o = 0
    q_lat = proj[:, o:o + q_lora]; o += q_lora
    kv_lat = proj[:, o:o + kv_lora]; o += kv_lora
    kpe_raw = proj[:, o:o + LANE]; o += LANE
    kpe_rot = proj[:, o:o + LANE]; o += LANE
    b_g = proj[:, o:o + conv_dim]; o += conv_dim
    c_g = proj[:, o:o + conv_dim]; o += conv_dim
    h_c = proj[:, o:o + conv_dim]; o += conv_dim
    u = proj[:, o:o + sgu_dim]; o += sgu_dim
    v = proj[:, o:o + sgu_dim]

    cos = cs_ref[:, :LANE]
    sin = cs_ref[:, LANE:]
    ckv_out[...] = _rms(kv_lat, kvg_ref[...])
    kpe_out[...] = (kpe_raw * cos + kpe_rot * sin)[:, :MLA_ROPE]

    qn = _rms(q_lat, qg_ref[...]).astype(BF16)
    raw = jnp.dot(qn, wq_ref[...], preferred_element_type=F32)
    rot = jnp.dot(qn, wqr_ref[...], preferred_element_type=F32)
    for hd in range(MLA_HEADS):
        a = hd * HEAD_K
        q_out[:, a:a + LANE] = (raw[:, a:a + LANE] * Q_SCALE).astype(BF16)
        pe = raw[:, a + LANE:a + HEAD_K] * cos + rot[:, hd * LANE:(hd + 1) * LANE] * sin
        q_out[:, a + LANE:a + HEAD_K] = (pe * Q_SCALE).astype(BF16)

    bg_out[...] = b_g
    z_out[...] = c_g * h_c
    u_out[...] = _gelu(u)
    v_out[...] = _ln(_gelu(v), lng_ref[...], lnb_ref[...])


def _in_proj(x_parts, mod3, cs, w_in_p, qg, kvg, wq_p, wq_rot, lng, lnb):
    n = sum(a.shape[0] for a in x_parts)
    d = x_parts[0].shape[1]
    q_lora, kv_lora = qg.shape[1], kvg.shape[1]
    sgu_dim = lng.shape[1]
    conv_dim = (w_in_p.shape[1] - q_lora - kv_lora - 2 * LANE - 2 * sgu_dim) // 3
    t = ROW_TILE
    nc = t // SGU_CHUNK
    row = lambda w: pl.BlockSpec((t, w), lambda i: (i, 0))
    mod = lambda k: pl.BlockSpec((nc, 1, d), lambda i: (i, 0, k))
    x_specs, x_starts = _part_specs(x_parts, t)
    kern = functools.partial(_in_kernel, x_starts=x_starts, q_lora=q_lora, kv_lora=kv_lora,
                             conv_dim=conv_dim, sgu_dim=sgu_dim)
    widths = (MLA_HEADS * HEAD_K, kv_lora, MLA_ROPE, conv_dim, conv_dim, sgu_dim, sgu_dim)
    dtypes = (BF16, F32, F32, F32, F32, F32, F32)
    return pl.pallas_call(
        kern,
        grid=(n // t,),
        in_specs=x_specs + [mod(0), mod(1), row(2 * LANE), _const_spec(w_in_p.shape),
                            _const_spec(qg.shape), _const_spec(kvg.shape), _const_spec(wq_p.shape),
                            _const_spec(wq_rot.shape), _const_spec(lng.shape), _const_spec(lnb.shape)],
        out_specs=[row(w) for w in widths],
        out_shape=[jax.ShapeDtypeStruct((n, w), dt) for w, dt in zip(widths, dtypes)],
        compiler_params=_cparams("arbitrary"),
        name="in_proj",
    )(*x_parts, mod3, mod3, cs, w_in_p, qg, kvg, wq_p, wq_rot, lng, lnb)


def _kv_kernel(ckv_ref, kpe_ref, wk_ref, wv_ref, k_out, v_out):
    c = ckv_ref[...].astype(BF16)
    kn = jnp.dot(c, wk_ref[...], preferred_element_type=F32)
    vv = jnp.dot(c, wv_ref[...], preferred_element_type=F32)
    kpe = kpe_ref[...].astype(BF16)
    ones = jnp.ones((c.shape[0], HEAD_V - MLA_V), BF16)
    zeros = jnp.zeros((c.shape[0], HEAD_K - MLA_NOPE - MLA_ROPE), BF16)
    for hd in range(MLA_HEADS):
        a = hd * HEAD_K
        k_out[:, a:a + MLA_NOPE] = kn[:, hd * MLA_NOPE:(hd + 1) * MLA_NOPE].astype(BF16)
        k_out[:, a + MLA_NOPE:a + MLA_NOPE + MLA_ROPE] = kpe
        k_out[:, a + MLA_NOPE + MLA_ROPE:a + HEAD_K] = zeros
        b = hd * HEAD_V
        v_out[:, b:b + MLA_V] = vv[:, hd * MLA_V:(hd + 1) * MLA_V].astype(BF16)
        v_out[:, b + MLA_V:b + HEAD_V] = ones


def _kv_expand(ckv, kpe, wk, wv):
    n, kv_lora = ckv.shape
    t = ROW_TILE
    row = lambda w: pl.BlockSpec((t, w), lambda i: (i, 0))
    return pl.pallas_call(
        _kv_kernel,
        grid=(n // t,),
        in_specs=[row(kv_lora), row(MLA_ROPE), _const_spec(wk.shape), _const_spec(wv.shape)],
        out_specs=[row(MLA_HEADS * HEAD_K), row(MLA_HEADS * HEAD_V)],
        out_shape=[jax.ShapeDtypeStruct((n, MLA_HEADS * HEAD_K), BF16),
                   jax.ShapeDtypeStruct((n, MLA_HEADS * HEAD_V), BF16)],
        compiler_params=_cparams("arbitrary"),
        name="kv_expand",
    )(ckv, kpe, wk, wv)


_NT = (((1,), (1,)), ((), ()))


def _attn_prompt_kernel(q_ref, k_ref, v_ref, o_ref, s0_ref, s1_ref, m_ref, acc_ref, *, tile):
    i = pl.program_id(2)
    heads = range(HEADS_PER_STEP)

    def scores(s_ref, j):
        off = pl.multiple_of(j * tile, tile)
        for g in heads:
            cols = slice(g * HEAD_K, (g + 1) * HEAD_K)
            s_ref[g] = lax.dot_general(q_ref[:, cols], k_ref[pl.ds(off, tile), cols], _NT,
                                       preferred_element_type=F32)

    def accumulate(s_ref, j, masked):
        off = pl.multiple_of(j * tile, tile)
        if masked:
            visible = (lax.broadcasted_iota(jnp.int32, (tile, tile), 1) // CHUNK
                       <= lax.broadcasted_iota(jnp.int32, (tile, tile), 0) // CHUNK)
        for g in heads:
            s = s_ref[g]
            if masked:
                s = jnp.where(visible, s, -jnp.inf)
            m_old = m_ref[g]
            m_new = jnp.maximum(m_old, jnp.max(s, axis=-1, keepdims=True))
            p = jnp.exp2(s - m_new).astype(BF16)
            pv = jnp.dot(p, v_ref[pl.ds(off, tile), g * HEAD_V:(g + 1) * HEAD_V],
                         preferred_element_type=F32)
            acc_ref[g] = jnp.exp2(m_old - m_new) * acc_ref[g] + pv
            m_ref[g] = m_new

    m_ref[...] = jnp.full(m_ref.shape, -jnp.inf, F32)
    acc_ref[...] = jnp.zeros(acc_ref.shape, F32)
    scores(s0_ref, 0)

    def pair(jj, c):
        scores(s1_ref, 2 * jj + 1)
        accumulate(s0_ref, 2 * jj, False)
        scores(s0_ref, 2 * jj + 2)
        accumulate(s1_ref, 2 * jj + 1, False)
        return c

    lax.fori_loop(0, i // 2, pair, 0)

    @pl.when(i % 2 == 0)
    def _():
        accumulate(s0_ref, i, True)

    @pl.when(i % 2 == 1)
    def _():
        scores(s1_ref, i)
        accumulate(s0_ref, i - 1, False)
        accumulate(s1_ref, i, True)

    for g in heads:
        acc = acc_ref[g]
        o_ref[:, g * MLA_V:(g + 1) * MLA_V] = (acc[:, :MLA_V] / acc[:, MLA_V:]).astype(o_ref.dtype)


def _attn_prompt(q, k, v, batch, seq):
    t = ATTN_TILE
    nq = seq // t
    g = HEADS_PER_STEP
    return pl.pallas_call(
        functools.partial(_attn_prompt_kernel, tile=t),
        grid=(batch, MLA_HEADS // g, nq),
        in_specs=[pl.BlockSpec((t, g * HEAD_K), lambda b, h, i: (b * nq + i, h)),
                  pl.BlockSpec((seq, g * HEAD_K), lambda b, h, i: (b, h)),
                  pl.BlockSpec((seq, g * HEAD_V), lambda b, h, i: (b, h))],
        out_specs=pl.BlockSpec((t, g * MLA_V), lambda b, h, i: (b * nq + i, h)),
        out_shape=jax.ShapeDtypeStruct((batch * seq, MLA_HEADS * MLA_V), BF16),
        scratch_shapes=[pltpu.VMEM((g, t, t), F32), pltpu.VMEM((g, t, t), F32),
                        pltpu.VMEM((g, t, 1), F32), pltpu.VMEM((g, t, HEAD_V), F32)],
        compiler_params=_cparams("arbitrary", "arbitrary", "arbitrary"),
        name="attn_prompt",
    )(q, k, v)


def _attn_sample_kernel(q_ref, kp_ref, vp_ref, kn_ref, vn_ref, o_ref, *, n_new):
    q = q_ref[...]
    s1 = lax.dot_general(q, kp_ref[...], _NT, preferred_element_type=F32)
    s2 = lax.dot_general(q, kn_ref[...], _NT, preferred_element_type=F32)
    s2 = jnp.where(lax.broadcasted_iota(jnp.int32, s2.shape, 1) < n_new, s2, -jnp.inf)
    m = jnp.maximum(jnp.max(s1, axis=-1, keepdims=True), jnp.max(s2, axis=-1, keepdims=True))
    acc = jnp.dot(jnp.exp2(s1 - m).astype(BF16), vp_ref[...], preferred_element_type=F32)
    acc = acc + jnp.dot(jnp.exp2(s2 - m).astype(BF16), vn_ref[...], preferred_element_type=F32)
    o_ref[...] = (acc[:, :MLA_V] / acc[:, MLA_V:]).astype(o_ref.dtype)


def _attn_sample(q, k_past, v_past, k, v, dec_batch, past, n_new, first_chunk):
    c = SGU_CHUNK
    new = lambda w: pl.BlockSpec((c, w), lambda b, h: (first_chunk + b, h))
    old = lambda w: pl.BlockSpec((past, w), lambda b, h: (b, h))
    return pl.pallas_call(
        functools.partial(_attn_sample_kernel, n_new=n_new),
        grid=(dec_batch, MLA_HEADS),
        in_specs=[new(HEAD_K), old(HEAD_K), old(HEAD_V), new(HEAD_K), new(HEAD_V)],
        out_specs=pl.BlockSpec((c, MLA_V), lambda b, h: (b, h)),
        out_shape=jax.ShapeDtypeStruct((dec_batch * c, MLA_HEADS * MLA_V), BF16),
        compiler_params=_cparams("arbitrary", "arbitrary"),
        name="attn_sample",
    )(q, k_past, v_past, k, v)


def _route(h, wr_ref, br_ref, n_exp):
    hi = h.astype(BF16)
    lo = (h - hi.astype(F32)).astype(BF16)
    a = jnp.dot(hi, wr_ref[...], preferred_element_type=F32)
    b = jnp.dot(lo, wr_ref[:, :LANE], preferred_element_type=F32)
    logits = a[:, :LANE] + (a[:, LANE:] + b) + br_ref[...]
    lane = lax.broadcasted_iota(jnp.int32, logits.shape, 1)
    lg = jnp.where(lane < n_exp, logits, -jnp.inf)
    v1 = jnp.max(lg, axis=-1, keepdims=True)
    i1 = jnp.min(jnp.where(lg == v1, lane, LANE), axis=-1, keepdims=True)
    lg2 = jnp.where(lane == i1, -jnp.inf, lg)
    v2 = jnp.max(lg2, axis=-1, keepdims=True)
    i2 = jnp.min(jnp.where(lg2 == v2, lane, LANE), axis=-1, keepdims=True)
    e = jnp.exp(v2 - v1)
    g1 = 1.0 / (1.0 + e)
    g2 = e / (1.0 + e)
    out = jnp.where(lane == 0, i1.astype(F32), 0.0)
    out = jnp.where(lane == 1, i2.astype(F32), out)
    out = jnp.where(lane == 2, g1, out)
    return jnp.where(lane == 3, g2, out)


def _pack_bf16_pairs(h):
    w = h.shape[1] // 2
    hi = pltpu.bitcast(h[:, :w].astype(BF16).astype(F32), U32)
    lo = pltpu.bitcast(h[:, w:].astype(BF16).astype(F32), U32)
    return (hi & jnp.uint32(0xFFFF0000)) | (lo >> 16)


def _unpack_bf16_pairs(w):
    hi = pltpu.bitcast(w & jnp.uint32(0xFFFF0000), F32).astype(BF16)
    lo = pltpu.bitcast(w << 16, F32).astype(BF16)
    return hi, lo


def _out_kernel(*refs, attn_starts, x_starts, alpha, n_exp, routed):
    na, nx = len(attn_starts), len(x_starts)
    attn_refs, x_refs, refs = refs[:na], refs[na:na + nx], refs[na + nx:]
    if routed:
        (z_ref, zprev_ref, bg_ref, u_ref, v_ref, cw_ref, sw_ref, sb_ref,
         g1_ref, sh2_ref, sc2_ref, lng_ref, lnb_ref, wo_ref, wr_ref, br_ref,
         x_out, h_out, route_out, mix_ref) = refs
    else:
        (z_ref, zprev_ref, bg_ref, u_ref, v_ref, cw_ref, sw_ref, sb_ref,
         g1_ref, sh2_ref, sc2_ref, lng_ref, lnb_ref, wo_ref,
         x_out, h_out, mix_ref) = refs
    t, d = x_refs[0].shape
    nc = t // SGU_CHUNK
    n_attn = attn_refs[0].shape[1]
    conv_dim = z_ref.shape[1]

    z = z_ref[...]
    zp = zprev_ref[...]
    pos = lax.broadcasted_iota(jnp.int32, (nc, SGU_CHUNK, conv_dim), 1)
    r1 = pltpu.roll(z, 1, 0).reshape(nc, SGU_CHUNK, conv_dim)
    r2 = pltpu.roll(z, 2, 0).reshape(nc, SGU_CHUNK, conv_dim)
    r1 = jnp.where(pos == 0, zp[:, 1:2, :], r1)
    r2 = jnp.where(pos == 0, zp[:, 0:1, :], jnp.where(pos == 1, zp[:, 1:2, :], r2))
    cw = cw_ref[...]
    y = cw[0:1, :] * r2 + cw[1:2, :] * r1 + cw[2:3, :] * z.reshape(nc, SGU_CHUNK, conv_dim)
    conv = bg_ref[...] * y.reshape(t, conv_dim)
    mix_ref[:, :n_attn] = _read_parts(attn_refs, attn_starts)
    mix_ref[:, n_attn:n_attn + conv_dim] = conv.astype(BF16)

    tri = (lax.broadcasted_iota(jnp.int32, (SGU_CHUNK, SGU_CHUNK), 0)
           >= lax.broadcasted_iota(jnp.int32, (SGU_CHUNK, SGU_CHUNK), 1))
    hw = v_ref.shape[1] // SGU_HEADS
    for hd in range(SGU_HEADS):
        w_m = jnp.where(tri, sw_ref[hd], 0.0).astype(BF16)
        bias = sb_ref[:, hd:hd + 1]
        for c in range(nc):
            rows = slice(c * SGU_CHUNK, (c + 1) * SGU_CHUNK)
            cols = slice(hd * hw, (hd + 1) * hw)
            mixed = jnp.dot(w_m, v_ref[rows, cols].astype(BF16), preferred_element_type=F32) + bias
            mix_ref[rows, n_attn + conv_dim + hd * hw:n_attn + conv_dim + (hd + 1) * hw] = (
                u_ref[rows, cols] * mixed).astype(BF16)

    o = jnp.dot(mix_ref[...], wo_ref[...], preferred_element_type=F32)
    x3 = _read_parts(x_refs, x_starts).reshape(nc, SGU_CHUNK, d)
    yres = (alpha * x3 + (1.0 + g1_ref[...]) * o.reshape(nc, SGU_CHUNK, d)).reshape(t, d)
    x1 = _ln(yres, lng_ref[...], lnb_ref[...])
    x_out[...] = x1
    h2 = (x1.reshape(nc, SGU_CHUNK, d) * (1.0 + sc2_ref[...]) + sh2_ref[...]).reshape(t, d)
    if routed:
        _store_token_tiles(h_out, _pack_bf16_pairs(h2))
        route_out[...] = _route(h2, wr_ref, br_ref, n_exp)
    else:
        h_out[...] = h2.astype(BF16)


def _mixer_out(attn_parts, z, zprev, bg, u, v, conv_w, sgu_w, sgu_bt, x_parts, mod3, lng, lnb,
               w_o, alpha, router=None):
    n, d = z.shape[0], x_parts[0].shape[1]
    t = ROW_TILE
    nc = t // SGU_CHUNK
    routed = router is not None
    row = lambda w: pl.BlockSpec((t, w), lambda i: (i, 0))
    mod = lambda k: pl.BlockSpec((nc, 1, d), lambda i: (i, 0, k))
    attn_specs, attn_starts = _part_specs(attn_parts, t)
    x_specs, x_starts = _part_specs(x_parts, t)
    in_specs = attn_specs + x_specs + [
        row(z.shape[1]), pl.BlockSpec((nc,) + zprev.shape[1:], lambda i: (i, 0, 0)),
        row(bg.shape[1]), row(u.shape[1]), row(v.shape[1]),
        _const_spec(conv_w.shape), _const_spec(sgu_w.shape), _const_spec(sgu_bt.shape),
        mod(2), mod(3), mod(4), _const_spec(lng.shape), _const_spec(lnb.shape),
        _const_spec(w_o.shape)]
    args = [*attn_parts, *x_parts, z, zprev, bg, u, v, conv_w, sgu_w, sgu_bt, mod3, mod3, mod3,
            lng, lnb, w_o]
    out_specs = [row(d)]
    out_shape = [jax.ShapeDtypeStruct((n, d), F32)]
    n_exp = 0
    if routed:
        w_r, b_r, n_exp = router
        in_specs += [_const_spec(w_r.shape), _const_spec(b_r.shape)]
        args += [w_r, b_r]
        hw = d // 2 // LANE
        out_specs += [pl.BlockSpec((t * hw, LANE), lambda i: (i, 0)), row(LANE)]
        out_shape += [jax.ShapeDtypeStruct((n * hw, LANE), U32), jax.ShapeDtypeStruct((n, LANE), F32)]
    else:
        out_specs += [row(d)]
        out_shape += [jax.ShapeDtypeStruct((n, d), BF16)]
    return pl.pallas_call(
        functools.partial(_out_kernel, attn_starts=attn_starts, x_starts=x_starts, alpha=alpha,
                          n_exp=n_exp, routed=routed),
        grid=(n // t,),
        in_specs=in_specs,
        out_specs=out_specs,
        out_shape=out_shape,
        scratch_shapes=[pltpu.VMEM((t, d), BF16)],
        compiler_params=_cparams("arbitrary"),
        name="mixer_out_routed" if routed else "mixer_out",
    )(*args)


def _ffn_kernel(h_ref, w1_ref, w3_ref, w2_ref, x_ref, g2_ref, lng_ref, lnb_ref, o_ref, acc_ref,
                *, alpha):
    j = pl.program_id(1)

    @pl.when(j == 0)
    def _():
        acc_ref[...] = jnp.zeros_like(acc_ref)

    h = h_ref[...]
    a = jnp.dot(h, w1_ref[...], preferred_element_type=F32)
    b = jnp.dot(h, w3_ref[...], preferred_element_type=F32)
    acc_ref[...] += jnp.dot((_silu(a) * b).astype(BF16), w2_ref[...], preferred_element_type=F32)

    @pl.when(j == pl.num_programs(1) - 1)
    def _():
        t, d = x_ref.shape
        nc = t // SGU_CHUNK
        x3 = x_ref[...].reshape(nc, SGU_CHUNK, d)
        y = alpha * x3 + (1.0 + g2_ref[...]) * acc_ref[...].reshape(nc, SGU_CHUNK, d)
        o_ref[...] = _ln(y.reshape(t, d), lng_ref[...], lnb_ref[...])


def _ffn_dense(h, w1, w3, w2, x, mod3, lng, lnb, alpha):
    n, d = x.shape
    f = w1.shape[1]
    t, tf = ROW_TILE, FF_TILE
    nc = t // SGU_CHUNK
    return pl.pallas_call(
        functools.partial(_ffn_kernel, alpha=alpha),
        grid=(n // t, f // tf),
        in_specs=[pl.BlockSpec((t, d), lambda i, j: (i, 0)),
                  pl.BlockSpec((d, tf), lambda i, j: (0, j)),
                  pl.BlockSpec((d, tf), lambda i, j: (0, j)),
                  pl.BlockSpec((tf, d), lambda i, j: (j, 0)),
                  pl.BlockSpec((t, d), lambda i, j: (i, 0)),
                  pl.BlockSpec((nc, 1, d), lambda i, j: (i, 0, 5)),
                  _const_spec(lng.shape), _const_spec(lnb.shape)],
        out_specs=pl.BlockSpec((t, d), lambda i, j: (i, 0)),
        out_shape=jax.ShapeDtypeStruct((n, d), F32),
        scratch_shapes=[pltpu.VMEM((t, d), F32)],
        compiler_params=_cparams("arbitrary", "arbitrary"),
        name="ffn_dense",
    )(h, w1, w3, w2, x, mod3, lng, lnb)


def _moe_kernel(te_ref, tv_ref, rt_ref, h_hbm, w1_ref, w3_ref, w2_ref, o_ref,
                hbuf, hb, acc_ref, sem, *, tm, n_tiles):
    r = pl.program_id(0)
    j = pl.program_id(1)
    half = hb.shape[1] // 2
    wt = half // LANE

    def row_copy(tile, slot, i):
        tok = rt_ref[tile * tm + i]
        return pltpu.make_async_copy(h_hbm.at[pl.ds(pl.multiple_of(tok * wt, wt), wt), :],
                                     hbuf.at[slot, pl.ds(pl.multiple_of(i * wt, wt), wt), :],
                                     sem.at[slot])

    def start_tile(tile, slot):
        def body(i, c):
            row_copy(tile, slot, i).start()
            return c
        lax.fori_loop(0, tm, body, 0, unroll=GATHER_UNROLL)

    def wait_tile(tile, slot):
        def body(i, c):
            row_copy(tile, slot, i).wait()
            return c
        lax.fori_loop(0, tm, body, 0, unroll=GATHER_UNROLL)

    slot = r % 2

    @pl.when(j == 0)
    def _():
        @pl.when(r == 0)
        def _():
            start_tile(0, 0)

        wait_tile(r, slot)

        @pl.when(r + 1 < n_tiles)
        def _():
            start_tile(r + 1, 1 - slot)

        for c in range(wt):
            hi, lo = _unpack_bf16_pairs(hbuf[slot, pl.ds(c, tm, stride=wt), :])
            hb[:, c * LANE:(c + 1) * LANE] = hi
            hb[:, half + c * LANE:half + (c + 1) * LANE] = lo
        acc_ref[...] = jnp.zeros_like(acc_ref)

    @pl.when(tv_ref[r] == 1)
    def _():
        h = hb[...]
        a = jnp.dot(h, w1_ref[...], preferred_element_type=F32)
        b = jnp.dot(h, w3_ref[...], preferred_element_type=F32)
        acc_ref[...] += jnp.dot((_silu(a) * b).astype(BF16), w2_ref[...],
                                preferred_element_type=F32)

    @pl.when(j == pl.num_programs(1) - 1)
    def _():
        _store_token_tiles(o_ref, acc_ref[...])


def _store_token_tiles(o_ref, x):
    t, w = x.shape
    wt = w // LANE
    for c in range(wt):
        o_ref[pl.ds(c, t, stride=wt), :] = x[:, c * LANE:(c + 1) * LANE]


def _moe_experts(h_packed, w1, w3, w2, tile_expert, tile_valid, row_token):
    n_exp, d, f = w1.shape
    p = row_token.shape[0]
    tm, tf = ROW_TILE, MOE_FF_TILE
    assert f % tf == 0
    n_tiles = p // tm
    nj = f // tf
    hw, ow = d // 2 // LANE, d // LANE
    col = lambda j, tv, r: j * tv[r] + (nj - 1) * (1 - tv[r])
    grid_spec = pltpu.PrefetchScalarGridSpec(
        num_scalar_prefetch=3,
        grid=(n_tiles, nj),
        in_specs=[pl.BlockSpec(memory_space=pl.ANY),
                  pl.BlockSpec((None, d, tf), lambda r, j, te, tv, rt: (te[r], 0, col(j, tv, r))),
                  pl.BlockSpec((None, d, tf), lambda r, j, te, tv, rt: (te[r], 0, col(j, tv, r))),
                  pl.BlockSpec((None, tf, d), lambda r, j, te, tv, rt: (te[r], col(j, tv, r), 0))],
        out_specs=pl.BlockSpec((tm * ow, LANE), lambda r, j, te, tv, rt: (r, 0)),
        scratch_shapes=[pltpu.VMEM((2, tm * hw, LANE), U32), pltpu.VMEM((tm, d), BF16),
                        pltpu.VMEM((tm, d), F32), pltpu.SemaphoreType.DMA((2,))],
    )
    return pl.pallas_call(
        functools.partial(_moe_kernel, tm=tm, n_tiles=n_tiles),
        grid_spec=grid_spec,
        out_shape=jax.ShapeDtypeStruct((p * ow, LANE), F32),
        compiler_params=_cparams("arbitrary", "arbitrary"),
        name="moe_experts",
    )(tile_expert, tile_valid, row_token, h_packed, w1, w3, w2)


def _combine_kernel(pos_ref, ys_hbm, route_ref, x_ref, g2_ref, lng_ref, lnb_ref, o_ref,
                    ybuf, fbuf, sem, *, tc, n_tiles, alpha):
    i = pl.program_id(0)
    t, d = x_ref.shape
    wt = d // LANE

    def row_copy(tile, slot, k, r):
        p = pos_ref[(tile * tc + r) * TOP_K + k]
        return pltpu.make_async_copy(ys_hbm.at[pl.ds(pl.multiple_of(p * wt, wt), wt), :],
                                     ybuf.at[slot, k, pl.ds(pl.multiple_of(r * wt, wt), wt), :],
                                     sem.at[slot])

    def start_tile(tile, slot):
        def body(r, c):
            for k in range(TOP_K):
                row_copy(tile, slot, k, r).start()
            return c
        lax.fori_loop(0, tc, body, 0, unroll=GATHER_UNROLL)

    def wait_tile(tile, slot):
        def body(r, c):
            for k in range(TOP_K):
                row_copy(tile, slot, k, r).wait()
            return c
        lax.fori_loop(0, tc, body, 0, unroll=GATHER_UNROLL)

    slot = i % 2

    @pl.when(i == 0)
    def _():
        start_tile(0, 0)

    wait_tile(i, slot)

    @pl.when(i + 1 < n_tiles)
    def _():
        start_tile(i + 1, 1 - slot)

    nc = t // SGU_CHUNK
    gates = [route_ref[:, TOP_K + k:TOP_K + k + 1] for k in range(TOP_K)]
    for c in range(wt):
        f = gates[0] * ybuf[slot, 0, pl.ds(c, t, stride=wt), :]
        for k in range(1, TOP_K):
            f = f + gates[k] * ybuf[slot, k, pl.ds(c, t, stride=wt), :]
        fbuf[:, c * LANE:(c + 1) * LANE] = f
    x3 = x_ref[...].reshape(nc, SGU_CHUNK, d)
    y = alpha * x3 + (1.0 + g2_ref[...]) * fbuf[...].reshape(nc, SGU_CHUNK, d)
    o_ref[...] = _ln(y.reshape(t, d), lng_ref[...], lnb_ref[...])


def _moe_combine(ys, pos, route, x, mod3, lng, lnb, alpha):
    n, d = x.shape
    tc = COMBINE_TILE
    nc = tc // SGU_CHUNK
    n_tiles = n // tc
    wt = d // LANE
    grid_spec = pltpu.PrefetchScalarGridSpec(
        num_scalar_prefetch=1,
        grid=(n_tiles,),
        in_specs=[pl.BlockSpec(memory_space=pl.ANY),
                  pl.BlockSpec((tc, LANE), lambda i, pos: (i, 0)),
                  pl.BlockSpec((tc, d), lambda i, pos: (i, 0)),
                  pl.BlockSpec((nc, 1, d), lambda i, pos: (i, 0, 5)),
                  _const_spec(lng.shape), _const_spec(lnb.shape)],
        out_specs=pl.BlockSpec((tc, d), lambda i, pos: (i, 0)),
        scratch_shapes=[pltpu.VMEM((2, TOP_K, tc * wt, LANE), F32), pltpu.VMEM((tc, d), F32),
                        pltpu.SemaphoreType.DMA((2,))],
    )
    return pl.pallas_call(
        functools.partial(_combine_kernel, tc=tc, n_tiles=n_tiles, alpha=alpha),
        grid_spec=grid_spec,
        out_shape=jax.ShapeDtypeStruct((n, d), F32),
        compiler_params=_cparams("arbitrary"),
        name="moe_combine",
    )(pos.reshape(-1), ys, route, x, mod3, lng, lnb)


def _dispatch_tables(route, real, n_exp, tm):
    n = real.shape[0]
    real2 = jnp.asarray(np.repeat(real, TOP_K))
    e = route[:, :TOP_K].astype(jnp.int32).reshape(-1)
    onehot = ((e[:, None] == jnp.arange(n_exp, dtype=jnp.int32)[None, :]) & real2[:, None]).astype(jnp.int32)
    csum = jnp.cumsum(onehot, axis=0)
    rank = jnp.take_along_axis(csum, e[:, None], axis=1)[:, 0] - 1
    counts = csum[-1]
    padded = ((counts + tm - 1) // tm) * tm
    ends = jnp.cumsum(padded)
    dest = (ends - padded)[e] + rank
    p = -(-(TOP_K * int(real.sum())) // tm) * tm + n_exp * tm
    tok = jnp.repeat(jnp.arange(n, dtype=jnp.int32), TOP_K)
    row_token = jnp.zeros((p,), jnp.int32).at[jnp.where(real2, dest, p)].set(tok, mode="drop")
    tile_start = jnp.arange(p // tm, dtype=jnp.int32) * tm
    tile_expert = jnp.minimum(jnp.sum(tile_start[:, None] >= ends[None, :], axis=1), n_exp - 1)
    tile_valid = (tile_start < ends[-1]).astype(jnp.int32)
    pos = jnp.where(real2, dest, 0)
    return tile_expert.astype(jnp.int32), tile_valid, row_token, pos


def _rot_cols(w):
    half = w.shape[-1] // 2
    return jnp.concatenate([-w[..., half:], w[..., :half]], axis=-1)


def _pad_cols(w, width):
    return jnp.pad(w, ((0, 0), (0, width - w.shape[1])))


def _layout_w_in(w, q_lora, kv_lora):
    o = q_lora + kv_lora
    kpe = w[:, o:o + MLA_ROPE]
    return jnp.concatenate([w[:, :o], _pad_cols(kpe, LANE), _pad_cols(_rot_cols(kpe), LANE),
                            w[:, o + MLA_ROPE:]], axis=1).astype(BF16)


def _layout_w_uq(w):
    per = MLA_NOPE + MLA_ROPE
    main, rot = [], []
    for hd in range(MLA_HEADS):
        pe = w[:, hd * per + MLA_NOPE:(hd + 1) * per]
        main += [w[:, hd * per:hd * per + MLA_NOPE], _pad_cols(pe, LANE)]
        rot += [_pad_cols(_rot_cols(pe), LANE)]
    return jnp.concatenate(main, axis=1).astype(BF16), jnp.concatenate(rot, axis=1).astype(BF16)


def _layout_w_ukv(w):
    per = MLA_NOPE + MLA_V
    wk = jnp.concatenate([w[:, hd * per:hd * per + MLA_NOPE] for hd in range(MLA_HEADS)], axis=1)
    wv = jnp.concatenate([w[:, hd * per + MLA_NOPE:(hd + 1) * per] for hd in range(MLA_HEADS)], axis=1)
    return wk.astype(BF16), wv.astype(BF16)


def _rope_table(pos):
    half = MLA_ROPE // 2
    inv = ROPE_THETA ** (-jnp.arange(half, dtype=F32) / half)
    ang = pos.astype(F32)[:, None] * inv[None, :]
    pad = jnp.zeros((pos.shape[0], LANE - MLA_ROPE), F32)
    cos, sin = jnp.cos(ang), jnp.sin(ang)
    return jnp.concatenate([cos, cos, pad, sin, sin, pad], axis=1)


def kernel(x_prompt, x_sample, cache_ckv, cache_kpe, state_conv, c_prompt, c_sample, w_ada, b_ada, w_in, q_norm_g, kv_norm_g, w_uq, w_ukv, conv_w, sgu_ln_g, sgu_ln_b, sgu_w, sgu_b, w_o, ln_g, ln_b, ffn_w1, ffn_w3, ffn_w2, router_w, router_b, exp_w1, exp_w3, exp_w2):
    batch, seq, d = x_prompt.shape
    dec_batch, dec_seq, _ = x_sample.shape
    depth, _, past, kv_lora = cache_ckv.shape
    q_lora = q_norm_g.shape[1]
    conv_dim = conv_w.shape[2]
    sgu_dim = sgu_ln_g.shape[1]
    n_exp = router_w.shape[2]
    c = SGU_CHUNK
    assert seq % ATTN_TILE == 0 and (batch * seq) % ROW_TILE == 0
    assert CONV_WIDTH - 1 <= dec_seq <= c and (dec_batch * c) % ROW_TILE == 0
    assert sgu_dim == SGU_HEADS * LANE and (dec_batch * past) % ROW_TILE == 0
    alpha = (2 * depth) ** 0.25

    n_p = batch * seq
    n_s = dec_batch * c
    n = n_p + n_s
    n_chunks = n // c
    chunk_batch = np.concatenate([np.arange(n_p // c) // (seq // c), batch + np.arange(dec_batch)])
    real = np.concatenate([np.ones(n_p, bool), np.tile(np.arange(c) < dec_seq, dec_batch)])
    pos = jnp.concatenate([jnp.tile(jnp.arange(seq), batch),
                           jnp.tile(past + jnp.arange(c), dec_batch)])
    cs = _rope_table(pos)

    x_parts = [x_prompt.reshape(n_p, d),
               jnp.pad(x_sample, ((0, 0), (0, c - dec_seq), (0, 0))).reshape(n_s, d)]
    n_c = batch + dec_batch
    c_all = jnp.pad(jnp.concatenate([c_prompt, c_sample], axis=0), ((0, -n_c % 8), (0, 0)))
    mod = _adaln_mod(c_all, w_ada, b_ada)

    outs = {k: [] for k in ("ckv_p", "kpe_p", "conv_p", "ckv_s", "kpe_s", "conv_s", "v_s")}
    keep = min(seq, past)
    for l in range(depth):
        mod3 = mod[l][chunk_batch].reshape(n_chunks, 1, 6 * d)
        w_in_p = _layout_w_in(w_in[l], q_lora, kv_lora)
        wq_p, wq_rot = _layout_w_uq(w_uq[l])
        wk, wv = _layout_w_ukv(w_ukv[l])
        row2 = lambda a: a.reshape(1, -1)

        q, ckv, kpe, z, bg, u, v = _in_proj(
            x_parts, mod3, cs, w_in_p, row2(q_norm_g[l]), row2(kv_norm_g[l]), wq_p, wq_rot,
            row2(sgu_ln_g[l]), row2(sgu_ln_b[l]))

        k_new, v_new = _kv_expand(ckv, kpe, wk, wv)
        k_past, v_past = _kv_expand(cache_ckv[l].reshape(dec_batch * past, kv_lora),
                                    cache_kpe[l].reshape(dec_batch * past, MLA_ROPE), wk, wv)
        attn_parts = [
            _attn_prompt(q, k_new, v_new, batch, seq),
            _attn_sample(q, k_past, v_past, k_new, v_new, dec_batch, past, dec_seq, n_p // c)]

        z_p = z[:n_p].reshape(batch, seq // c, c, conv_dim)[:, :, c - (CONV_WIDTH - 1):, :]
        zprev = jnp.concatenate([
            jnp.concatenate([jnp.zeros_like(z_p[:, :1]), z_p[:, :-1]], axis=1).reshape(n_p // c, CONV_WIDTH - 1, conv_dim),
            state_conv[l]], axis=0)

        moe = l % 2 == 1
        i = l // 2
        router = None
        if moe:
            w_r = _pad_cols(router_w[i], LANE)
            w_r_hi = w_r.astype(BF16)
            w_r_lo = (w_r - w_r_hi.astype(F32)).astype(BF16)
            router = (jnp.concatenate([w_r_hi, w_r_lo], axis=1),
                      _pad_cols(router_b[i].reshape(1, n_exp), LANE), n_exp)
        res = _mixer_out(attn_parts, z, zprev, bg, u, v, conv_w[l], sgu_w[l].astype(F32), sgu_b[l].T,
                         x_parts, mod3, row2(ln_g[l, 0]), row2(ln_b[l, 0]), w_o[l].astype(BF16),
                         alpha, router=router)
        if moe:
            x1, h2, route = res
            te, tv, row_token, rpos = _dispatch_tables(route, real, n_exp, ROW_TILE)
            ys = _moe_experts(h2, exp_w1[i].astype(BF16), exp_w3[i].astype(BF16),
                              exp_w2[i].astype(BF16), te, tv, row_token)
            x = _moe_combine(ys, rpos, route, x1, mod3, row2(ln_g[l, 1]), row2(ln_b[l, 1]), alpha)
        else:
            x1, h2 = res
            x = _ffn_dense(h2, ffn_w1[i].astype(BF16), ffn_w3[i].astype(BF16),
                           ffn_w2[i].astype(BF16), x1, mod3, row2(ln_g[l, 1]), row2(ln_b[l, 1]), alpha)
        x_parts = [x]

        sample = lambda a: a[n_p:].reshape(dec_batch, c, a.shape[1])[:, :dec_seq]
        outs["ckv_p"].append(ckv[:n_p].reshape(batch, seq, kv_lora)[:, seq - keep:])
        outs["kpe_p"].append(kpe[:n_p].reshape(batch, seq, MLA_ROPE)[:, seq - keep:])
        outs["conv_p"].append(z[:n_p].reshape(batch, seq, conv_dim)[:, seq - (CONV_WIDTH - 1):])
        outs["ckv_s"].append(sample(ckv))
        outs["kpe_s"].append(sample(kpe))
        outs["conv_s"].append(sample(z)[:, dec_seq - (CONV_WIDTH - 1):])
        outs["v_s"].append(sample(v))

    y_p = x[:n_p].reshape(batch, seq, d)
    y_s = x[n_p:].reshape(dec_batch, c, d)[:, :dec_seq]
    st = lambda k: jnp.stack(outs[k])
    return (y_p, y_s, st("ckv_p"), st("kpe_p"), st("conv_p"), st("ckv_s"), st("kpe_s"),
            st("conv_s"), st("v_s"))
```

```python
import functools

import numpy as np
import jax
import jax.numpy as jnp
from jax import lax
from jax.experimental import pallas as pl
from jax.experimental.pallas import tpu as pltpu

F32 = jnp.float32
BF16 = jnp.bfloat16
U32 = jnp.uint32

CHUNK = 64
MLA_HEADS = 8
MLA_NOPE = 128
MLA_ROPE = 64
MLA_V = 128
ROPE_THETA = 10000.0
ATTN_SCALE = (MLA_NOPE + MLA_ROPE) ** -0.5
Q_SCALE = ATTN_SCALE * float(np.log2(np.e))
CONV_WIDTH = 3
SGU_HEADS = 4
SGU_CHUNK = 128
TOP_K = 2
LN_EPS = 1e-5
RMS_EPS = 1e-6

LANE = 128
VMEM_LIMIT = 56 * 2 ** 20

HEAD_K = 2 * LANE
HEAD_V = 2 * LANE
HEADS_PER_STEP = 2
VT_ROWS = MLA_V + 16
ROW_TILE = 512
FF_TILE = 512
MOE_FF_TILE = 1024
ATTN_TILE = 512
COMBINE_TILE = 256
GATHER_UNROLL = 8


def _cparams(*sem):
    return pltpu.CompilerParams(dimension_semantics=sem, vmem_limit_bytes=VMEM_LIMIT)


def _const_spec(shape):
    nd = len(shape)
    return pl.BlockSpec(shape, lambda *_: (0,) * nd, pipeline_mode=pl.Buffered(1))


def _rms(x, g):
    return x * lax.rsqrt(jnp.mean(x * x, axis=-1, keepdims=True) + RMS_EPS) * g


def _ln(x, g, b):
    mu = jnp.mean(x, axis=-1, keepdims=True)
    xc = x - mu
    var = jnp.mean(xc * xc, axis=-1, keepdims=True)
    return xc * lax.rsqrt(var + LN_EPS) * g + b


def _gelu(x):
    return 0.5 * x * (1.0 + lax.erf(x * (2.0 ** -0.5)))


def _silu(x):
    return x * jax.nn.sigmoid(x)


def _per_chunk(x, fn):
    t, w = x.shape
    return fn(x.reshape(t // SGU_CHUNK, SGU_CHUNK, w)).reshape(t, w)


def _mod_kernel(c_ref, w_ref, b_ref, o_ref):
    a = _silu(c_ref[...]).astype(BF16)
    o_ref[...] = jnp.dot(a, w_ref[...].astype(BF16), preferred_element_type=F32) + b_ref[...]


def _adaln_mod(c_all, w_ada, b_ada):
    depth, d, m = w_ada.shape
    r = c_all.shape[0]
    tn = m // 8
    return pl.pallas_call(
        _mod_kernel,
        grid=(depth, m // tn),
        in_specs=[pl.BlockSpec((r, d), lambda l, j: (0, 0)),
                  pl.BlockSpec((None, d, tn), lambda l, j: (l, 0, j)),
                  pl.BlockSpec((None, 1, tn), lambda l, j: (l, 0, j))],
        out_specs=pl.BlockSpec((None, r, tn), lambda l, j: (l, 0, j)),
        out_shape=jax.ShapeDtypeStruct((depth, r, m), F32),
        compiler_params=_cparams("arbitrary", "arbitrary"),
        name="adaln_mod",
    )(c_all, w_ada, b_ada.reshape(depth, 1, m))


def _part_specs(parts, t):
    specs, starts, start = [], [], 0
    for a in parts:
        tiles = a.shape[0] // t
        specs.append(pl.BlockSpec((t, a.shape[1]),
                                  lambda i, s=start, m=tiles - 1: (jnp.clip(i - s, 0, m), 0)))
        starts.append(start)
        start += tiles
    return specs, tuple(starts)


def _read_parts(refs, starts):
    i = pl.program_id(0)
    x = refs[0][...]
    for ref, s in zip(refs[1:], starts[1:]):
        x = jnp.where(i >= s, ref[...], x)
    return x


def _in_kernel(*refs, x_starts, q_lora, kv_lora, conv_dim, sgu_dim):
    nx = len(x_starts)
    (sh_ref, sc_ref, cs_ref, w_in_ref, qg_ref, kvg_ref, wq_ref, wqr_ref, lng_ref, lnb_ref,
     q_out, ckv_out, kpe_out, z_out, bg_out, u_out, v_out) = refs[nx:]
    t, d = refs[0].shape
    nc = t // SGU_CHUNK
    x3 = _read_parts(refs[:nx], x_starts).reshape(nc, SGU_CHUNK, d)
    h = (x3 * (1.0 + sc_ref[...]) + sh_ref[...]).reshape(t, d)
    proj = jnp.dot(h.astype(BF16), w_in_ref[...], preferred_element_type=F32)

    o = 0
    q_lat = proj[:, o:o + q_lora]; o += q_lora
    kv_lat = proj[:, o:o + kv_lora]; o += kv_lora
    kpe_raw = proj[:, o:o + LANE]; o += LANE
    kpe_rot = proj[:, o:o + LANE]; o += LANE
    b_g = proj[:, o:o + conv_dim]; o += conv_dim
    c_g = proj[:, o:o + conv_dim]; o += conv_dim
    h_c = proj[:, o:o + conv_dim]; o += conv_dim
    u = proj[:, o:o + sgu_dim]; o += sgu_dim
    v = proj[:, o:o + sgu_dim]

    cos = cs_ref[:, :LANE]
    sin = cs_ref[:, LANE:]
    ckv_out[...] = _rms(kv_lat, kvg_ref[...])
    kpe_out[...] = (kpe_raw * cos + kpe_rot * sin)[:, :MLA_ROPE]

    qn = _rms(q_lat, qg_ref[...]).astype(BF16)
    raw = jnp.dot(qn, wq_ref[...], preferred_element_type=F32)
    rot = jnp.dot(qn, wqr_ref[...], preferred_element_type=F32)
    for hd in range(MLA_HEADS):
        a = hd * HEAD_K
        q_out[:, a:a + LANE] = (raw[:, a:a + LANE] * Q_SCALE).astype(BF16)
        pe = raw[:, a + LANE:a + HEAD_K] * cos + rot[:, hd * LANE:(hd + 1) * LANE] * sin
        q_out[:, a + LANE:a + HEAD_K] = (pe * Q_SCALE).astype(BF16)

    bg_out[...] = b_g
    z_out[...] = c_g * h_c
    u_out[...] = _gelu(u)
    v_out[...] = _ln(_gelu(v), lng_ref[...], lnb_ref[...])


def _in_proj(x_parts, mod3, cs, w_in_p, qg, kvg, wq_p, wq_rot, lng, lnb):
    n = sum(a.shape[0] for a in x_parts)
    d = x_parts[0].shape[1]
    q_lora, kv_lora = qg.shape[1], kvg.shape[1]
    sgu_dim = lng.shape[1]
    conv_dim = (w_in_p.shape[1] - q_lora - kv_lora - 2 * LANE - 2 * sgu_dim) // 3
    t = ROW_TILE
    nc = t // SGU_CHUNK
    row = lambda w: pl.BlockSpec((t, w), lambda i: (i, 0))
    mod = lambda k: pl.BlockSpec((nc, 1, d), lambda i: (i, 0, k))
    x_specs, x_starts = _part_specs(x_parts, t)
    kern = functools.partial(_in_kernel, x_starts=x_starts, q_lora=q_lora, kv_lora=kv_lora,
                             conv_dim=conv_dim, sgu_dim=sgu_dim)
    widths = (MLA_HEADS * HEAD_K, kv_lora, MLA_ROPE, conv_dim, conv_dim, sgu_dim, sgu_dim)
    dtypes = (BF16, F32, F32, F32, F32, F32, F32)
    return pl.pallas_call(
        kern,
        grid=(n // t,),
        in_specs=x_specs + [mod(0), mod(1), row(2 * LANE), _const_spec(w_in_p.shape),
                            _const_spec(qg.shape), _const_spec(kvg.shape), _const_spec(wq_p.shape),
                            _const_spec(wq_rot.shape), _const_spec(lng.shape), _const_spec(lnb.shape)],
        out_specs=[row(w) for w in widths],
        out_shape=[jax.ShapeDtypeStruct((n, w), dt) for w, dt in zip(widths, dtypes)],
        compiler_params=_cparams("arbitrary"),
        name="in_proj",
    )(*x_parts, mod3, mod3, cs, w_in_p, qg, kvg, wq_p, wq_rot, lng, lnb)


_NT = (((1,), (1,)), ((), ()))


def _kv_kernel(ckv_ref, kpe_ref, wk_ref, wv_ref, k_out, v_out):
    _kv_rows(ckv_ref[...].astype(BF16), kpe_ref, wk_ref, wv_ref, k_out, v_out)


def _kv_vt_kernel(ckv_ref, kpe_ref, wk_ref, wv_ref, wvt_ref, k_out, v_out, vt_out):
    c = ckv_ref[...].astype(BF16)
    _kv_rows(c, kpe_ref, wk_ref, wv_ref, k_out, v_out)
    vt = lax.dot_general(wvt_ref[...], c, _NT, preferred_element_type=F32)
    ones_t = jnp.ones((VT_ROWS - MLA_V, c.shape[0]), BF16)
    for hd in range(MLA_HEADS):
        vt_out[hd * VT_ROWS:hd * VT_ROWS + MLA_V, :] = vt[hd * MLA_V:(hd + 1) * MLA_V, :].astype(BF16)
        vt_out[hd * VT_ROWS + MLA_V:(hd + 1) * VT_ROWS, :] = ones_t


def _kv_rows(c, kpe_ref, wk_ref, wv_ref, k_out, v_out):
    kn = jnp.dot(c, wk_ref[...], preferred_element_type=F32)
    vv = jnp.dot(c, wv_ref[...], preferred_element_type=F32)
    kpe = kpe_ref[...].astype(BF16)
    ones = jnp.ones((c.shape[0], HEAD_V - MLA_V), BF16)
    zeros = jnp.zeros((c.shape[0], HEAD_K - MLA_NOPE - MLA_ROPE), BF16)
    for hd in range(MLA_HEADS):
        a = hd * HEAD_K
        k_out[:, a:a + MLA_NOPE] = kn[:, hd * MLA_NOPE:(hd + 1) * MLA_NOPE].astype(BF16)
        k_out[:, a + MLA_NOPE:a + MLA_NOPE + MLA_ROPE] = kpe
        k_out[:, a + MLA_NOPE + MLA_ROPE:a + HEAD_K] = zeros
        b = hd * HEAD_V
        v_out[:, b:b + MLA_V] = vv[:, hd * MLA_V:(hd + 1) * MLA_V].astype(BF16)
        v_out[:, b + MLA_V:b + HEAD_V] = ones


def _kv_expand(ckv, kpe, wk, wv, wvt=None):
    n, kv_lora = ckv.shape
    t = ROW_TILE
    row = lambda w: pl.BlockSpec((t, w), lambda i: (i, 0))
    in_specs = [row(kv_lora), row(MLA_ROPE), _const_spec(wk.shape), _const_spec(wv.shape)]
    out_specs = [row(MLA_HEADS * HEAD_K), row(MLA_HEADS * HEAD_V)]
    out_shape = [jax.ShapeDtypeStruct((n, MLA_HEADS * HEAD_K), BF16),
                 jax.ShapeDtypeStruct((n, MLA_HEADS * HEAD_V), BF16)]
    args = [ckv, kpe, wk, wv]
    if wvt is not None:
        in_specs.append(_const_spec(wvt.shape))
        args.append(wvt)
        out_specs.append(pl.BlockSpec((None, MLA_HEADS * VT_ROWS, t), lambda i: (i, 0, 0)))
        out_shape.append(jax.ShapeDtypeStruct((n // t, MLA_HEADS * VT_ROWS, t), BF16))
    return pl.pallas_call(
        _kv_kernel if wvt is None else _kv_vt_kernel,
        grid=(n // t,),
        in_specs=in_specs,
        out_specs=out_specs,
        out_shape=out_shape,
        compiler_params=_cparams("arbitrary"),
        name="kv_expand" if wvt is None else "kv_expand_vt",
    )(*args)


def _attn_prompt_kernel(q_ref, k_ref, vt_ref, o_ref, s0_ref, s1_ref, m_ref, acc_ref, *, tile):
    i = pl.program_id(2)
    heads = range(HEADS_PER_STEP)

    def scores(s_ref, j):
        off = pl.multiple_of(j * tile, tile)
        for g in heads:
            cols = slice(g * HEAD_K, (g + 1) * HEAD_K)
            s_ref[g] = lax.dot_general(k_ref[pl.ds(off, tile), cols], q_ref[:, cols], _NT,
                                       preferred_element_type=F32)

    def accumulate(s_ref, j, masked):
        if masked:
            visible = (lax.broadcasted_iota(jnp.int32, (tile, tile), 0) // CHUNK
                       <= lax.broadcasted_iota(jnp.int32, (tile, tile), 1) // CHUNK)
        for g in heads:
            s = s_ref[g]
            if masked:
                s = jnp.where(visible, s, -jnp.inf)
            m_old = m_ref[g]
            m_new = jnp.maximum(m_old, jnp.max(s, axis=0, keepdims=True))
            p = jnp.exp2(s - m_new).astype(BF16)
            pv = jnp.dot(vt_ref[j, g * VT_ROWS:(g + 1) * VT_ROWS, :], p,
                         preferred_element_type=F32)
            acc_ref[g] = jnp.exp2(m_old - m_new) * acc_ref[g] + pv
            m_ref[g] = m_new

    m_ref[...] = jnp.full(m_ref.shape, -jnp.inf, F32)
    acc_ref[...] = jnp.zeros(acc_ref.shape, F32)
    scores(s0_ref, 0)

    def pair(jj, c):
        scores(s1_ref, 2 * jj + 1)
        accumulate(s0_ref, 2 * jj, False)
        scores(s0_ref, 2 * jj + 2)
        accumulate(s1_ref, 2 * jj + 1, False)
        return c

    lax.fori_loop(0, i // 2, pair, 0)

    @pl.when(i % 2 == 0)
    def _():
        accumulate(s0_ref, i, True)

    @pl.when(i % 2 == 1)
    def _():
        scores(s1_ref, i)
        accumulate(s0_ref, i - 1, False)
        accumulate(s1_ref, i, True)

    for g in heads:
        acc = acc_ref[g]
        o_t = acc[:MLA_V] / acc[MLA_V:MLA_V + 1]
        o_ref[:, g * MLA_V:(g + 1) * MLA_V] = o_t.T.astype(o_ref.dtype)


def _attn_prompt(q, k, vt, batch, seq):
    t = ATTN_TILE
    assert t == vt.shape[2]
    nq = seq // t
    g = HEADS_PER_STEP
    return pl.pallas_call(
        functools.partial(_attn_prompt_kernel, tile=t),
        grid=(batch, MLA_HEADS // g, nq),
        in_specs=[pl.BlockSpec((t, g * HEAD_K), lambda b, h, i: (b * nq + i, h)),
                  pl.BlockSpec((seq, g * HEAD_K), lambda b, h, i: (b, h)),
                  pl.BlockSpec((nq, g * VT_ROWS, t), lambda b, h, i: (b, h, 0))],
        out_specs=pl.BlockSpec((t, g * MLA_V), lambda b, h, i: (b * nq + i, h)),
        out_shape=jax.ShapeDtypeStruct((batch * seq, MLA_HEADS * MLA_V), BF16),
        scratch_shapes=[pltpu.VMEM((g, t, t), F32), pltpu.VMEM((g, t, t), F32),
                        pltpu.VMEM((g, 1, t), F32), pltpu.VMEM((g, VT_ROWS, t), F32)],
        compiler_params=_cparams("arbitrary", "arbitrary", "arbitrary"),
        name="attn_prompt",
    )(q, k, vt)


def _attn_sample_kernel(q_ref, kp_ref, vp_ref, kn_ref, vn_ref, o_ref, *, n_new):
    q = q_ref[...]
    s1 = lax.dot_general(q, kp_ref[...], _NT, preferred_element_type=F32)
    s2 = lax.dot_general(q, kn_ref[...], _NT, preferred_element_type=F32)
    s2 = jnp.where(lax.broadcasted_iota(jnp.int32, s2.shape, 1) < n_new, s2, -jnp.inf)
    m = jnp.maximum(jnp.max(s1, axis=-1, keepdims=True), jnp.max(s2, axis=-1, keepdims=True))
    acc = jnp.dot(jnp.exp2(s1 - m).astype(BF16), vp_ref[...], preferred_element_type=F32)
    acc = acc + jnp.dot(jnp.exp2(s2 - m).astype(BF16), vn_ref[...], preferred_element_type=F32)
    o_ref[...] = (acc[:, :MLA_V] / acc[:, MLA_V:]).astype(o_ref.dtype)


def _attn_sample(q, k_past, v_past, k, v, dec_batch, past, n_new, first_chunk):
    c = SGU_CHUNK
    new = lambda w: pl.BlockSpec((c, w), lambda b, h: (first_chunk + b, h))
    old = lambda w: pl.BlockSpec((past, w), lambda b, h: (b, h))
    return pl.pallas_call(
        functools.partial(_attn_sample_kernel, n_new=n_new),
        grid=(dec_batch, MLA_HEADS),
        in_specs=[new(HEAD_K), old(HEAD_K), old(HEAD_V), new(HEAD_K), new(HEAD_V)],
        out_specs=pl.BlockSpec((c, MLA_V), lambda b, h: (b, h)),
        out_shape=jax.ShapeDtypeStruct((dec_batch * c, MLA_HEADS * MLA_V), BF16),
        compiler_params=_cparams("arbitrary", "arbitrary"),
        name="attn_sample",
    )(q, k_past, v_past, k, v)


def _route(h, wr_ref, br_ref, n_exp):
    hi = h.astype(BF16)
    lo = (h - hi.astype(F32)).astype(BF16)
    a = jnp.dot(hi, wr_ref[...], preferred_element_type=F32)
    b = jnp.dot(lo, wr_ref[:, :LANE], preferred_element_type=F32)
    logits = a[:, :LANE] + (a[:, LANE:] + b) + br_ref[...]
    lane = lax.broadcasted_iota(jnp.int32, logits.shape, 1)
    lg = jnp.where(lane < n_exp, logits, -jnp.inf)
    v1 = jnp.max(lg, axis=-1, keepdims=True)
    i1 = jnp.min(jnp.where(lg == v1, lane, LANE), axis=-1, keepdims=True)
    lg2 = jnp.where(lane == i1, -jnp.inf, lg)
    v2 = jnp.max(lg2, axis=-1, keepdims=True)
    i2 = jnp.min(jnp.where(lg2 == v2, lane, LANE), axis=-1, keepdims=True)
    e = jnp.exp(v2 - v1)
    g1 = 1.0 / (1.0 + e)
    g2 = e / (1.0 + e)
    out = jnp.where(lane == 0, i1.astype(F32), 0.0)
    out = jnp.where(lane == 1, i2.astype(F32), out)
    out = jnp.where(lane == 2, g1, out)
    return jnp.where(lane == 3, g2, out)


def _pack_bf16_pairs(h):
    w = h.shape[1] // 2
    hi = pltpu.bitcast(h[:, :w].astype(BF16).astype(F32), U32)
    lo = pltpu.bitcast(h[:, w:].astype(BF16).astype(F32), U32)
    return (hi & jnp.uint32(0xFFFF0000)) | (lo >> 16)


def _unpack_bf16_pairs(w):
    hi = pltpu.bitcast(w & jnp.uint32(0xFFFF0000), F32).astype(BF16)
    lo = pltpu.bitcast(w << 16, F32).astype(BF16)
    return hi, lo


def _out_kernel(*refs, attn_starts, x_starts, alpha, n_exp, routed):
    na, nx = len(attn_starts), len(x_starts)
    attn_refs, x_refs, refs = refs[:na], refs[na:na + nx], refs[na + nx:]
    if routed:
        (z_ref, zprev_ref, bg_ref, u_ref, v_ref, cw_ref, sw_ref, sb_ref,
         g1_ref, sh2_ref, sc2_ref, lng_ref, lnb_ref, wo_ref, wr_ref, br_ref,
         x_out, h_out, route_out, mix_ref) = refs
    else:
        (z_ref, zprev_ref, bg_ref, u_ref, v_ref, cw_ref, sw_ref, sb_ref,
         g1_ref, sh2_ref, sc2_ref, lng_ref, lnb_ref, wo_ref,
         x_out, h_out, mix_ref) = refs
    t, d = x_refs[0].shape
    nc = t // SGU_CHUNK
    n_attn = attn_refs[0].shape[1]
    conv_dim = z_ref.shape[1]

    z = z_ref[...]
    zp = zprev_ref[...]
    pos = lax.broadcasted_iota(jnp.int32, (nc, SGU_CHUNK, conv_dim), 1)
    r1 = pltpu.roll(z, 1, 0).reshape(nc, SGU_CHUNK, conv_dim)
    r2 = pltpu.roll(z, 2, 0).reshape(nc, SGU_CHUNK, conv_dim)
    r1 = jnp.where(pos == 0, zp[:, 1:2, :], r1)
    r2 = jnp.where(pos == 0, zp[:, 0:1, :], jnp.where(pos == 1, zp[:, 1:2, :], r2))
    cw = cw_ref[...]
    y = cw[0:1, :] * r2 + cw[1:2, :] * r1 + cw[2:3, :] * z.reshape(nc, SGU_CHUNK, conv_dim)
    conv = bg_ref[...] * y.reshape(t, conv_dim)
    mix_ref[:, :n_attn] = _read_parts(attn_refs, attn_starts)
    mix_ref[:, n_attn:n_attn + conv_dim] = conv.astype(BF16)

    tri = (lax.broadcasted_iota(jnp.int32, (SGU_CHUNK, SGU_CHUNK), 0)
           >= lax.broadcasted_iota(jnp.int32, (SGU_CHUNK, SGU_CHUNK), 1))
    hw = v_ref.shape[1] // SGU_HEADS
    for hd in range(SGU_HEADS):
        w_m = jnp.where(tri, sw_ref[hd], 0.0).astype(BF16)
        bias = sb_ref[:, hd:hd + 1]
        for c in range(nc):
            rows = slice(c * SGU_CHUNK, (c + 1) * SGU_CHUNK)
            cols = slice(hd * hw, (hd + 1) * hw)
            mixed = jnp.dot(w_m, v_ref[rows, cols].astype(BF16), preferred_element_type=F32) + bias
            mix_ref[rows, n_attn + conv_dim + hd * hw:n_attn + conv_dim + (hd + 1) * hw] = (
                u_ref[rows, cols] * mixed).astype(BF16)

    o = jnp.dot(mix_ref[...], wo_ref[...], preferred_element_type=F32)
    x3 = _read_parts(x_refs, x_starts).reshape(nc, SGU_CHUNK, d)
    yres = (alpha * x3 + (1.0 + g1_ref[...]) * o.reshape(nc, SGU_CHUNK, d)).reshape(t, d)
    x1 = _ln(yres, lng_ref[...], lnb_ref[...])
    x_out[...] = x1
    h2 = (x1.reshape(nc, SGU_CHUNK, d) * (1.0 + sc2_ref[...]) + sh2_ref[...]).reshape(t, d)
    if routed:
        _store_token_tiles(h_out, _pack_bf16_pairs(h2))
        route_out[...] = _route(h2, wr_ref, br_ref, n_exp)
    else:
        h_out[...] = h2.astype(BF16)


def _mixer_out(attn_parts, z, zprev, bg, u, v, conv_w, sgu_w, sgu_bt, x_parts, mod3, lng, lnb,
               w_o, alpha, router=None):
    n, d = z.shape[0], x_parts[0].shape[1]
    t = ROW_TILE
    nc = t // SGU_CHUNK
    routed = router is not None
    row = lambda w: pl.BlockSpec((t, w), lambda i: (i, 0))
    mod = lambda k: pl.BlockSpec((nc, 1, d), lambda i: (i, 0, k))
    attn_specs, attn_starts = _part_specs(attn_parts, t)
    x_specs, x_starts = _part_specs(x_parts, t)
    in_specs = attn_specs + x_specs + [
        row(z.shape[1]), pl.BlockSpec((nc,) + zprev.shape[1:], lambda i: (i, 0, 0)),
        row(bg.shape[1]), row(u.shape[1]), row(v.shape[1]),
        _const_spec(conv_w.shape), _const_spec(sgu_w.shape), _const_spec(sgu_bt.shape),
        mod(2), mod(3), mod(4), _const_spec(lng.shape), _const_spec(lnb.shape),
        _const_spec(w_o.shape)]
    args = [*attn_parts, *x_parts, z, zprev, bg, u, v, conv_w, sgu_w, sgu_bt, mod3, mod3, mod3,
            lng, lnb, w_o]
    out_specs = [row(d)]
    out_shape = [jax.ShapeDtypeStruct((n, d), F32)]
    n_exp = 0
    if routed:
        w_r, b_r, n_exp = router
        in_specs += [_const_spec(w_r.shape), _const_spec(b_r.shape)]
        args += [w_r, b_r]
        hw = d // 2 // LANE
        out_specs += [pl.BlockSpec((t * hw, LANE), lambda i: (i, 0)), row(LANE)]
        out_shape += [jax.ShapeDtypeStruct((n * hw, LANE), U32), jax.ShapeDtypeStruct((n, LANE), F32)]
    else:
        out_specs += [row(d)]
        out_shape += [jax.ShapeDtypeStruct((n, d), BF16)]
    return pl.pallas_call(
        functools.partial(_out_kernel, attn_starts=attn_starts, x_starts=x_starts, alpha=alpha,
                          n_exp=n_exp, routed=routed),
        grid=(n // t,),
        in_specs=in_specs,
        out_specs=out_specs,
        out_shape=out_shape,
        scratch_shapes=[pltpu.VMEM((t, d), BF16)],
        compiler_params=_cparams("arbitrary"),
        name="mixer_out_routed" if routed else "mixer_out",
    )(*args)


def _ffn_kernel(h_ref, w1_ref, w3_ref, w2_ref, x_ref, g2_ref, lng_ref, lnb_ref, o_ref, acc_ref,
                *, alpha):
    j = pl.program_id(1)

    @pl.when(j == 0)
    def _():
        acc_ref[...] = jnp.zeros_like(acc_ref)

    h = h_ref[...]
    a = jnp.dot(h, w1_ref[...], preferred_element_type=F32)
    b = jnp.dot(h, w3_ref[...], preferred_element_type=F32)
    acc_ref[...] += jnp.dot((_silu(a) * b).astype(BF16), w2_ref[...], preferred_element_type=F32)

    @pl.when(j == pl.num_programs(1) - 1)
    def _():
        t, d = x_ref.shape
        nc = t // SGU_CHUNK
        x3 = x_ref[...].reshape(nc, SGU_CHUNK, d)
        y = alpha * x3 + (1.0 + g2_ref[...]) * acc_ref[...].reshape(nc, SGU_CHUNK, d)
        o_ref[...] = _ln(y.reshape(t, d), lng_ref[...], lnb_ref[...])


def _ffn_dense(h, w1, w3, w2, x, mod3, lng, lnb, alpha):
    n, d = x.shape
    f = w1.shape[1]
    t, tf = ROW_TILE, FF_TILE
    nc = t // SGU_CHUNK
    return pl.pallas_call(
        functools.partial(_ffn_kernel, alpha=alpha),
        grid=(n // t, f // tf),
        in_specs=[pl.BlockSpec((t, d), lambda i, j: (i, 0)),
                  pl.BlockSpec((d, tf), lambda i, j: (0, j)),
                  pl.BlockSpec((d, tf), lambda i, j: (0, j)),
                  pl.BlockSpec((tf, d), lambda i, j: (j, 0)),
                  pl.BlockSpec((t, d), lambda i, j: (i, 0)),
                  pl.BlockSpec((nc, 1, d), lambda i, j: (i, 0, 5)),
                  _const_spec(lng.shape), _const_spec(lnb.shape)],
        out_specs=pl.BlockSpec((t, d), lambda i, j: (i, 0)),
        out_shape=jax.ShapeDtypeStruct((n, d), F32),
        scratch_shapes=[pltpu.VMEM((t, d), F32)],
        compiler_params=_cparams("arbitrary", "arbitrary"),
        name="ffn_dense",
    )(h, w1, w3, w2, x, mod3, lng, lnb)


def _moe_kernel(te_ref, tv_ref, rt_ref, h_hbm, w1_ref, w3_ref, w2_ref, o_ref,
                hbuf, hb, acc_ref, sem, *, tm, n_tiles):
    r = pl.program_id(0)
    j = pl.program_id(1)
    half = hb.shape[1] // 2
    wt = half // LANE

    def row_copy(tile, slot, i):
        tok = rt_ref[tile * tm + i]
        return pltpu.make_async_copy(h_hbm.at[pl.ds(pl.multiple_of(tok * wt, wt), wt), :],
                                     hbuf.at[slot, pl.ds(pl.multiple_of(i * wt, wt), wt), :],
                                     sem.at[slot])

    def start_tile(tile, slot):
        def body(i, c):
            row_copy(tile, slot, i).start()
            return c
        lax.fori_loop(0, tm, body, 0, unroll=GATHER_UNROLL)

    def wait_tile(tile, slot):
        def body(i, c):
            row_copy(tile, slot, i).wait()
            return c
        lax.fori_loop(0, tm, body, 0, unroll=GATHER_UNROLL)

    slot = r % 2

    @pl.when(j == 0)
    def _():
        @pl.when(r == 0)
        def _():
            start_tile(0, 0)

        wait_tile(r, slot)

        @pl.when(r + 1 < n_tiles)
        def _():
            start_tile(r + 1, 1 - slot)

        for c in range(wt):
            hi, lo = _unpack_bf16_pairs(hbuf[slot, pl.ds(c, tm, stride=wt), :])
            hb[:, c * LANE:(c + 1) * LANE] = hi
            hb[:, half + c * LANE:half + (c + 1) * LANE] = lo
        acc_ref[...] = jnp.zeros_like(acc_ref)

    @pl.when(tv_ref[r] == 1)
    def _():
        h = hb[...]
        a = jnp.dot(h, w1_ref[...], preferred_element_type=F32)
        b = jnp.dot(h, w3_ref[...], preferred_element_type=F32)
        acc_ref[...] += jnp.dot((_silu(a) * b).astype(BF16), w2_ref[...],
                                preferred_element_type=F32)

    @pl.when(j == pl.num_programs(1) - 1)
    def _():
        _store_token_tiles(o_ref, acc_ref[...])


def _store_token_tiles(o_ref, x):
    t, w = x.shape
    wt = w // LANE
    for c in range(wt):
        o_ref[pl.ds(c, t, stride=wt), :] = x[:, c * LANE:(c + 1) * LANE]


def _moe_experts(h_packed, w1, w3, w2, tile_expert, tile_valid, row_token):
    n_exp, d, f = w1.shape
    p = row_token.shape[0]
    tm, tf = ROW_TILE, MOE_FF_TILE
    assert f % tf == 0
    n_tiles = p // tm
    nj = f // tf
    hw, ow = d // 2 // LANE, d // LANE
    col = lambda j, tv, r: j * tv[r] + (nj - 1) * (1 - tv[r])
    grid_spec = pltpu.PrefetchScalarGridSpec(
        num_scalar_prefetch=3,
        grid=(n_tiles, nj),
        in_specs=[pl.BlockSpec(memory_space=pl.ANY),
                  pl.BlockSpec((None, d, tf), lambda r, j, te, tv, rt: (te[r], 0, col(j, tv, r))),
                  pl.BlockSpec((None, d, tf), lambda r, j, te, tv, rt: (te[r], 0, col(j, tv, r))),
                  pl.BlockSpec((None, tf, d), lambda r, j, te, tv, rt: (te[r], col(j, tv, r), 0))],
        out_specs=pl.BlockSpec((tm * ow, LANE), lambda r, j, te, tv, rt: (r, 0)),
        scratch_shapes=[pltpu.VMEM((2, tm * hw, LANE), U32), pltpu.VMEM((tm, d), BF16),
                        pltpu.VMEM((tm, d), F32), pltpu.SemaphoreType.DMA((2,))],
    )
    return pl.pallas_call(
        functools.partial(_moe_kernel, tm=tm, n_tiles=n_tiles),
        grid_spec=grid_spec,
        out_shape=jax.ShapeDtypeStruct((p * ow, LANE), F32),
        compiler_params=_cparams("arbitrary", "arbitrary"),
        name="moe_experts",
    )(tile_expert, tile_valid, row_token, h_packed, w1, w3, w2)


def _combine_kernel(pos_ref, ys_hbm, route_ref, x_ref, g2_ref, lng_ref, lnb_ref, o_ref,
                    ybuf, fbuf, sem, *, tc, n_tiles, alpha):
    i = pl.program_id(0)
    t, d = x_ref.shape
    wt = d // LANE

    def row_copy(tile, slot, k, r):
        p = pos_ref[(tile * tc + r) * TOP_K + k]
        return pltpu.make_async_copy(ys_hbm.at[pl.ds(pl.multiple_of(p * wt, wt), wt), :],
                                     ybuf.at[slot, k, pl.ds(pl.multiple_of(r * wt, wt), wt), :],
                                     sem.at[slot])

    def start_tile(tile, slot):
        def body(r, c):
            for k in range(TOP_K):
                row_copy(tile, slot, k, r).start()
            return c
        lax.fori_loop(0, tc, body, 0, unroll=GATHER_UNROLL)

    def wait_tile(tile, slot):
        def body(r, c):
            for k in range(TOP_K):
                row_copy(tile, slot, k, r).wait()
            return c
        lax.fori_loop(0, tc, body, 0, unroll=GATHER_UNROLL)

    slot = i % 2

    @pl.when(i == 0)
    def _():
        start_tile(0, 0)

    wait_tile(i, slot)

    @pl.when(i + 1 < n_tiles)
    def _():
        start_tile(i + 1, 1 - slot)

    nc = t // SGU_CHUNK
    gates = [route_ref[:, TOP_K + k:TOP_K + k + 1] for k in range(TOP_K)]
    for c in range(wt):
        f = gates[0] * ybuf[slot, 0, pl.ds(c, t, stride=wt), :]
        for k in range(1, TOP_K):
            f = f + gates[k] * ybuf[slot, k, pl.ds(c, t, stride=wt), :]
        fbuf[:, c * LANE:(c + 1) * LANE] = f
    x3 = x_ref[...].reshape(nc, SGU_CHUNK, d)
    y = alpha * x3 + (1.0 + g2_ref[...]) * fbuf[...].reshape(nc, SGU_CHUNK, d)
    o_ref[...] = _ln(y.reshape(t, d), lng_ref[...], lnb_ref[...])


def _moe_combine(ys, pos, route, x, mod3, lng, lnb, alpha):
    n, d = x.shape
    tc = COMBINE_TILE
    nc = tc // SGU_CHUNK
    n_tiles = n // tc
    wt = d // LANE
    grid_spec = pltpu.PrefetchScalarGridSpec(
        num_scalar_prefetch=1,
        grid=(n_tiles,),
        in_specs=[pl.BlockSpec(memory_space=pl.ANY),
                  pl.BlockSpec((tc, LANE), lambda i, pos: (i, 0)),
                  pl.BlockSpec((tc, d), lambda i, pos: (i, 0)),
                  pl.BlockSpec((nc, 1, d), lambda i, pos: (i, 0, 5)),
                  _const_spec(lng.shape), _const_spec(lnb.shape)],
        out_specs=pl.BlockSpec((tc, d), lambda i, pos: (i, 0)),
        scratch_shapes=[pltpu.VMEM((2, TOP_K, tc * wt, LANE), F32), pltpu.VMEM((tc, d), F32),
                        pltpu.SemaphoreType.DMA((2,))],
    )
    return pl.pallas_call(
        functools.partial(_combine_kernel, tc=tc, n_tiles=n_tiles, alpha=alpha),
        grid_spec=grid_spec,
        out_shape=jax.ShapeDtypeStruct((n, d), F32),
        compiler_params=_cparams("arbitrary"),
        name="moe_combine",
    )(pos.reshape(-1), ys, route, x, mod3, lng, lnb)


def _dispatch_tables(route, real, n_exp, tm):
    n = real.shape[0]
    real2 = jnp.asarray(np.repeat(real, TOP_K))
    e = route[:, :TOP_K].astype(jnp.int32).reshape(-1)
    onehot = ((e[:, None] == jnp.arange(n_exp, dtype=jnp.int32)[None, :]) & real2[:, None]).astype(jnp.int32)
    csum = jnp.cumsum(onehot, axis=0)
    rank = jnp.take_along_axis(csum, e[:, None], axis=1)[:, 0] - 1
    counts = csum[-1]
    padded = ((counts + tm - 1) // tm) * tm
    ends = jnp.cumsum(padded)
    dest = (ends - padded)[e] + rank
    p = -(-(TOP_K * int(real.sum())) // tm) * tm + n_exp * tm
    tok = jnp.repeat(jnp.arange(n, dtype=jnp.int32), TOP_K)
    row_token = jnp.zeros((p,), jnp.int32).at[jnp.where(real2, dest, p)].set(tok, mode="drop")
    tile_start = jnp.arange(p // tm, dtype=jnp.int32) * tm
    tile_expert = jnp.minimum(jnp.sum(tile_start[:, None] >= ends[None, :], axis=1), n_exp - 1)
    tile_valid = (tile_start < ends[-1]).astype(jnp.int32)
    pos = jnp.where(real2, dest, 0)
    return tile_expert.astype(jnp.int32), tile_valid, row_token, pos


def _rot_cols(w):
    half = w.shape[-1] // 2
    return jnp.concatenate([-w[..., half:], w[..., :half]], axis=-1)


def _pad_cols(w, width):
    return jnp.pad(w, ((0, 0), (0, width - w.shape[1])))


def _layout_w_in(w, q_lora, kv_lora):
    o = q_lora + kv_lora
    kpe = w[:, o:o + MLA_ROPE]
    return jnp.concatenate([w[:, :o], _pad_cols(kpe, LANE), _pad_cols(_rot_cols(kpe), LANE),
                            w[:, o + MLA_ROPE:]], axis=1).astype(BF16)


def _layout_w_uq(w):
    per = MLA_NOPE + MLA_ROPE
    main, rot = [], []
    for hd in range(MLA_HEADS):
        pe = w[:, hd * per + MLA_NOPE:(hd + 1) * per]
        main += [w[:, hd * per:hd * per + MLA_NOPE], _pad_cols(pe, LANE)]
        rot += [_pad_cols(_rot_cols(pe), LANE)]
    return jnp.concatenate(main, axis=1).astype(BF16), jnp.concatenate(rot, axis=1).astype(BF16)


def _layout_w_ukv(w):
    per = MLA_NOPE + MLA_V
    wk = jnp.concatenate([w[:, hd * per:hd * per + MLA_NOPE] for hd in range(MLA_HEADS)], axis=1)
    wv = jnp.concatenate([w[:, hd * per + MLA_NOPE:(hd + 1) * per] for hd in range(MLA_HEADS)], axis=1)
    return wk.astype(BF16), wv.astype(BF16)


def _rope_table(pos):
    half = MLA_ROPE // 2
    inv = ROPE_THETA ** (-jnp.arange(half, dtype=F32) / half)
    ang = pos.astype(F32)[:, None] * inv[None, :]
    pad = jnp.zeros((pos.shape[0], LANE - MLA_ROPE), F32)
    cos, sin = jnp.cos(ang), jnp.sin(ang)
    return jnp.concatenate([cos, cos, pad, sin, sin, pad], axis=1)


def kernel(x_prompt, x_sample, cache_ckv, cache_kpe, state_conv, c_prompt, c_sample, w_ada, b_ada, w_in, q_norm_g, kv_norm_g, w_uq, w_ukv, conv_w, sgu_ln_g, sgu_ln_b, sgu_w, sgu_b, w_o, ln_g, ln_b, ffn_w1, ffn_w3, ffn_w2, router_w, router_b, exp_w1, exp_w3, exp_w2):
    batch, seq, d = x_prompt.shape
    dec_batch, dec_seq, _ = x_sample.shape
    depth, _, past, kv_lora = cache_ckv.shape
    q_lora = q_norm_g.shape[1]
    conv_dim = conv_w.shape[2]
    sgu_dim = sgu_ln_g.shape[1]
    n_exp = router_w.shape[2]
    c = SGU_CHUNK
    assert seq % ATTN_TILE == 0 and (batch * seq) % ROW_TILE == 0
    assert CONV_WIDTH - 1 <= dec_seq <= c and (dec_batch * c) % ROW_TILE == 0
    assert sgu_dim == SGU_HEADS * LANE and (dec_batch * past) % ROW_TILE == 0
    alpha = (2 * depth) ** 0.25

    n_p = batch * seq
    n_s = dec_batch * c
    n = n_p + n_s
    n_chunks = n // c
    chunk_batch = np.concatenate([np.arange(n_p // c) // (seq // c), batch + np.arange(dec_batch)])
    real = np.concatenate([np.ones(n_p, bool), np.tile(np.arange(c) < dec_seq, dec_batch)])
    pos = jnp.concatenate([jnp.tile(jnp.arange(seq), batch),
                           jnp.tile(past + jnp.arange(c), dec_batch)])
    cs = _rope_table(pos)

    x_parts = [x_prompt.reshape(n_p, d),
               jnp.pad(x_sample, ((0, 0), (0, c - dec_seq), (0, 0))).reshape(n_s, d)]
    n_c = batch + dec_batch
    c_all = jnp.pad(jnp.concatenate([c_prompt, c_sample], axis=0), ((0, -n_c % 8), (0, 0)))
    mod = _adaln_mod(c_all, w_ada, b_ada)

    outs = {k: [] for k in ("ckv_p", "kpe_p", "conv_p", "ckv_s", "kpe_s", "conv_s", "v_s")}
    keep = min(seq, past)
    for l in range(depth):
        mod3 = mod[l][chunk_batch].reshape(n_chunks, 1, 6 * d)
        w_in_p = _layout_w_in(w_in[l], q_lora, kv_lora)
        wq_p, wq_rot = _layout_w_uq(w_uq[l])
        wk, wv = _layout_w_ukv(w_ukv[l])
        row2 = lambda a: a.reshape(1, -1)

        q, ckv, kpe, z, bg, u, v = _in_proj(
            x_parts, mod3, cs, w_in_p, row2(q_norm_g[l]), row2(kv_norm_g[l]), wq_p, wq_rot,
            row2(sgu_ln_g[l]), row2(sgu_ln_b[l]))

        k_new, v_new, vt_new = _kv_expand(ckv, kpe, wk, wv, wv.T)
        k_past, v_past = _kv_expand(cache_ckv[l].reshape(dec_batch * past, kv_lora),
                                    cache_kpe[l].reshape(dec_batch * past, MLA_ROPE), wk, wv)
        attn_parts = [
            _attn_prompt(q, k_new, vt_new, batch, seq),
            _attn_sample(q, k_past, v_past, k_new, v_new, dec_batch, past, dec_seq, n_p // c)]

        z_p = z[:n_p].reshape(batch, seq // c, c, conv_dim)[:, :, c - (CONV_WIDTH - 1):, :]
        zprev = jnp.concatenate([
            jnp.concatenate([jnp.zeros_like(z_p[:, :1]), z_p[:, :-1]], axis=1).reshape(n_p // c, CONV_WIDTH - 1, conv_dim),
            state_conv[l]], axis=0)

        moe = l % 2 == 1
        i = l // 2
        router = None
        if moe:
            w_r = _pad_cols(router_w[i], LANE)
            w_r_hi = w_r.astype(BF16)
            w_r_lo = (w_r - w_r_hi.astype(F32)).astype(BF16)
            router = (jnp.concatenate([w_r_hi, w_r_lo], axis=1),
                      _pad_cols(router_b[i].reshape(1, n_exp), LANE), n_exp)
        res = _mixer_out(attn_parts, z, zprev, bg, u, v, conv_w[l], sgu_w[l].astype(F32), sgu_b[l].T,
                         x_parts, mod3, row2(ln_g[l, 0]), row2(ln_b[l, 0]), w_o[l].astype(BF16),
                         alpha, router=router)
        if moe:
            x1, h2, route = res
            te, tv, row_token, rpos = _dispatch_tables(route, real, n_exp, ROW_TILE)
            ys = _moe_experts(h2, exp_w1[i].astype(BF16), exp_w3[i].astype(BF16),
                              exp_w2[i].astype(BF16), te, tv, row_token)
            x = _moe_combine(ys, rpos, route, x1, mod3, row2(ln_g[l, 1]), row2(ln_b[l, 1]), alpha)
        else:
            x1, h2 = res
            x = _ffn_dense(h2, ffn_w1[i].astype(BF16), ffn_w3[i].astype(BF16),
                           ffn_w2[i].astype(BF16), x1, mod3, row2(ln_g[l, 1]), row2(ln_b[l, 1]), alpha)
        x_parts = [x]

        sample = lambda a: a[n_p:].reshape(dec_batch, c, a.shape[1])[:, :dec_seq]
        outs["ckv_p"].append(ckv[:n_p].reshape(batch, seq, kv_lora)[:, seq - keep:])
        outs["kpe_p"].append(kpe[:n_p].reshape(batch, seq, MLA_ROPE)[:, seq - keep:])
        outs["conv_p"].append(z[:n_p].reshape(batch, seq, conv_dim)[:, seq - (CONV_WIDTH - 1):])
        outs["ckv_s"].append(sample(ckv))
        outs["kpe_s"].append(sample(kpe))
        outs["conv_s"].append(sample(z)[:, dec_seq - (CONV_WIDTH - 1):])
        outs["v_s"].append(sample(v))

    y_p = x[:n_p].reshape(batch, seq, d)
    y_s = x[n_p:].reshape(dec_batch, c, d)[:, :dec_seq]
    st = lambda k: jnp.stack(outs[k])
    return (y_p, y_s, st("ckv_p"), st("kpe_p"), st("conv_p"), st("ckv_s"), st("kpe_s"),
            st("conv_s"), st("v_s"))
```

```python
import functools

import numpy as np
import jax
import jax.numpy as jnp
from jax import lax
from jax.experimental import pallas as pl
from jax.experimental.pallas import tpu as pltpu

F32 = jnp.float32
BF16 = jnp.bfloat16
U32 = jnp.uint32

CHUNK = 64
MLA_HEADS = 8
MLA_NOPE = 128
MLA_ROPE = 64
MLA_V = 128
ROPE_THETA = 10000.0
ATTN_SCALE = (MLA_NOPE + MLA_ROPE) ** -0.5
Q_SCALE = ATTN_SCALE * float(np.log2(np.e))
CONV_WIDTH = 3
SGU_HEADS = 4
SGU_CHUNK = 128
TOP_K = 2
LN_EPS = 1e-5
RMS_EPS = 1e-6

LANE = 128
VMEM_LIMIT = 56 * 2 ** 20

HEAD_K = 2 * LANE
HEAD_V = 2 * LANE
HEADS_PER_STEP = 2
ROW_TILE = 512
FF_TILE = 512
MOE_FF_TILE = 1024
ATTN_TILE = 512
COMBINE_TILE = 256
GATHER_UNROLL = 8


def _cparams(*sem):
    return pltpu.CompilerParams(dimension_semantics=sem, vmem_limit_bytes=VMEM_LIMIT)


def _const_spec(shape):
    nd = len(shape)
    return pl.BlockSpec(shape, lambda *_: (0,) * nd, pipeline_mode=pl.Buffered(1))


def _rms(x, g):
    return x * lax.rsqrt(jnp.mean(x * x, axis=-1, keepdims=True) + RMS_EPS) * g


def _ln(x, g, b):
    mu = jnp.mean(x, axis=-1, keepdims=True)
    xc = x - mu
    var = jnp.mean(xc * xc, axis=-1, keepdims=True)
    return xc * lax.rsqrt(var + LN_EPS) * g + b


def _gelu(x):
    return 0.5 * x * (1.0 + lax.erf(x * (2.0 ** -0.5)))


def _silu(x):
    return x * jax.nn.sigmoid(x)


def _per_chunk(x, fn):
    t, w = x.shape
    return fn(x.reshape(t // SGU_CHUNK, SGU_CHUNK, w)).reshape(t, w)


def _mod_kernel(c_ref, w_ref, b_ref, o_ref):
    a = _silu(c_ref[...]).astype(BF16)
    o_ref[...] = jnp.dot(a, w_ref[...].astype(BF16), preferred_element_type=F32) + b_ref[...]


def _adaln_mod(c_all, w_ada, b_ada):
    depth, d, m = w_ada.shape
    r = c_all.shape[0]
    tn = m // 8
    return pl.pallas_call(
        _mod_kernel,
        grid=(depth, m // tn),
        in_specs=[pl.BlockSpec((r, d), lambda l, j: (0, 0)),
                  pl.BlockSpec((None, d, tn), lambda l, j: (l, 0, j)),
                  pl.BlockSpec((None, 1, tn), lambda l, j: (l, 0, j))],
        out_specs=pl.BlockSpec((None, r, tn), lambda l, j: (l, 0, j)),
        out_shape=jax.ShapeDtypeStruct((depth, r, m), F32),
        compiler_params=_cparams("arbitrary", "arbitrary"),
        name="adaln_mod",
    )(c_all, w_ada, b_ada.reshape(depth, 1, m))


def _part_specs(parts, t):
    specs, starts, start = [], [], 0
    for a in parts:
        tiles = a.shape[0] // t
        specs.append(pl.BlockSpec((t, a.shape[1]),
                                  lambda i, s=start, m=tiles - 1: (jnp.clip(i - s, 0, m), 0)))
        starts.append(start)
        start += tiles
    return specs, tuple(starts)


def _read_parts(refs, starts):
    i = pl.program_id(0)
    x = refs[0][...]
    for ref, s in zip(refs[1:], starts[1:]):
        x = jnp.where(i >= s, ref[...], x)
    return x


def _in_kernel(*refs, x_starts, q_lora, kv_lora, conv_dim, sgu_dim):
    nx = len(x_starts)
    (sh_ref, sc_ref, cs_ref, w_in_ref, qg_ref, kvg_ref, wq_ref, wqr_ref, lng_ref, lnb_ref,
     q_out, ckv_out, kpe_out, z_out, bg_out, u_out, v_out) = refs[nx:]
    t, d = refs[0].shape
    nc = t // SGU_CHUNK
    x3 = _read_parts(refs[:nx], x_starts).reshape(nc, SGU_CHUNK, d)
    h = (x3 * (1.0 + sc_ref[...]) + sh_ref[...]).reshape(t, d)
    proj = jnp.dot(h.astype(BF16), w_in_ref[...], preferred_element_type=F32)

    o = 0
    q_lat = proj[:, o:o + q_lora]; o += q_lora
    kv_lat = proj[:, o:o + kv_lora]; o += kv_lora
    kpe_raw = proj[:, o:o + LANE]; o += LANE
    kpe_rot = proj[:, o:o + LANE]; o += LANE
    b_g = proj[:, o:o + conv_dim]; o += conv_dim
    c_g = proj[:, o:o + conv_dim]; o += conv_dim
    h_c = proj[:, o:o + conv_dim]; o += conv_dim
    u = proj[:, o:o + sgu_dim]; o += sgu_dim
    v = proj[:, o:o + sgu_dim]

    cos = cs_ref[:, :LANE]
    sin = cs_ref[:, LANE:]
    ckv_out[...] = _rms(kv_lat, kvg_ref[...])
    kpe_out[...] = (kpe_raw * cos + kpe_rot * sin)[:, :MLA_ROPE]

    qn = _rms(q_lat, qg_ref[...]).astype(BF16)
    raw = jnp.dot(qn, wq_ref[...], preferred_element_type=F32)
    rot = jnp.dot(qn, wqr_ref[...], preferred_element_type=F32)
    for hd in range(MLA_HEADS):
        a = hd * HEAD_K
        q_out[:, a:a + LANE] = (raw[:, a:a + LANE] * Q_SCALE).astype(BF16)
        pe = raw[:, a + LANE:a + HEAD_K] * cos + rot[:, hd * LANE:(hd + 1) * LANE] * sin
        q_out[:, a + LANE:a + HEAD_K] = (pe * Q_SCALE).astype(BF16)

    bg_out[...] = b_g
    z_out[...] = c_g * h_c
    u_out[...] = _gelu(u)
    v_out[...] = _ln(_gelu(v), lng_ref[...], lnb_ref[...])


def _in_proj(x_parts, mod3, cs, w_in_p, qg, kvg, wq_p, wq_rot, lng, lnb):
    n = sum(a.shape[0] for a in x_parts)
    d = x_parts[0].shape[1]
    q_lora, kv_lora = qg.shape[1], kvg.shape[1]
    sgu_dim = lng.shape[1]
    conv_dim = (w_in_p.shape[1] - q_lora - kv_lora - 2 * LANE - 2 * sgu_dim) // 3
    t = ROW_TILE
    nc = t // SGU_CHUNK
    row = lambda w: pl.BlockSpec((t, w), lambda i: (i, 0))
    mod = lambda k: pl.BlockSpec((nc, 1, d), lambda i: (i, 0, k))
    x_specs, x_starts = _part_specs(x_parts, t)
    kern = functools.partial(_in_kernel, x_starts=x_starts, q_lora=q_lora, kv_lora=kv_lora,
                             conv_dim=conv_dim, sgu_dim=sgu_dim)
    widths = (MLA_HEADS * HEAD_K, kv_lora, MLA_ROPE, conv_dim, conv_dim, sgu_dim, sgu_dim)
    dtypes = (BF16, F32, F32, F32, F32, F32, F32)
    return pl.pallas_call(
        kern,
        grid=(n // t,),
        in_specs=x_specs + [mod(0), mod(1), row(2 * LANE), _const_spec(w_in_p.shape),
                            _const_spec(qg.shape), _const_spec(kvg.shape), _const_spec(wq_p.shape),
                            _const_spec(wq_rot.shape), _const_spec(lng.shape), _const_spec(lnb.shape)],
        out_specs=[row(w) for w in widths],
        out_shape=[jax.ShapeDtypeStruct((n, w), dt) for w, dt in zip(widths, dtypes)],
        compiler_params=_cparams("arbitrary"),
        name="in_proj",
    )(*x_parts, mod3, mod3, cs, w_in_p, qg, kvg, wq_p, wq_rot, lng, lnb)


_NT = (((1,), (1,)), ((), ()))


def _kv_kernel(ckv_ref, kpe_ref, wk_ref, wv_ref, k_out, v_out):
    c = ckv_ref[...].astype(BF16)
    kn = jnp.dot(c, wk_ref[...], preferred_element_type=F32)
    vv = jnp.dot(c, wv_ref[...], preferred_element_type=F32)
    kpe = kpe_ref[...].astype(BF16)
    ones = jnp.ones((c.shape[0], HEAD_V - MLA_V), BF16)
    zeros = jnp.zeros((c.shape[0], HEAD_K - MLA_NOPE - MLA_ROPE), BF16)
    for hd in range(MLA_HEADS):
        a = hd * HEAD_K
        k_out[:, a:a + MLA_NOPE] = kn[:, hd * MLA_NOPE:(hd + 1) * MLA_NOPE].astype(BF16)
        k_out[:, a + MLA_NOPE:a + MLA_NOPE + MLA_ROPE] = kpe
        k_out[:, a + MLA_NOPE + MLA_ROPE:a + HEAD_K] = zeros
        b = hd * HEAD_V
        v_out[:, b:b + MLA_V] = vv[:, hd * MLA_V:(hd + 1) * MLA_V].astype(BF16)
        v_out[:, b + MLA_V:b + HEAD_V] = ones


def _kv_expand(ckv, kpe, wk, wv):
    n, kv_lora = ckv.shape
    t = ROW_TILE
    row = lambda w: pl.BlockSpec((t, w), lambda i: (i, 0))
    return pl.pallas_call(
        _kv_kernel,
        grid=(n // t,),
        in_specs=[row(kv_lora), row(MLA_ROPE), _const_spec(wk.shape), _const_spec(wv.shape)],
        out_specs=[row(MLA_HEADS * HEAD_K), row(MLA_HEADS * HEAD_V)],
        out_shape=[jax.ShapeDtypeStruct((n, MLA_HEADS * HEAD_K), BF16),
                   jax.ShapeDtypeStruct((n, MLA_HEADS * HEAD_V), BF16)],
        compiler_params=_cparams("arbitrary"),
        name="kv_expand",
    )(ckv, kpe, wk, wv)


def _attn_prompt_kernel(q_ref, k_ref, v_ref, o_ref, s0_ref, s1_ref, m_ref, acc_ref, *, tile):
    i = pl.program_id(2)
    heads = range(HEADS_PER_STEP)

    def scores(s_ref, j):
        off = pl.multiple_of(j * tile, tile)
        for g in heads:
            cols = slice(g * HEAD_K, (g + 1) * HEAD_K)
            s_ref[g] = lax.dot_general(q_ref[:, cols], k_ref[pl.ds(off, tile), cols], _NT,
                                       preferred_element_type=F32)

    def accumulate(s_ref, j, masked):
        off = pl.multiple_of(j * tile, tile)
        if masked:
            visible = (lax.broadcasted_iota(jnp.int32, (tile, tile), 1) // CHUNK
                       <= lax.broadcasted_iota(jnp.int32, (tile, tile), 0) // CHUNK)
        for g in heads:
            s = s_ref[g]
            if masked:
                s = jnp.where(visible, s, -jnp.inf)
            m_old = m_ref[g]
            m_new = jnp.maximum(m_old, jnp.max(s, axis=-1, keepdims=True))
            p = jnp.exp2(s - m_new).astype(BF16)
            pv = jnp.dot(p, v_ref[pl.ds(off, tile), g * HEAD_V:(g + 1) * HEAD_V],
                         preferred_element_type=F32)
            acc_ref[g] = jnp.exp2(m_old - m_new) * acc_ref[g] + pv
            m_ref[g] = m_new

    m_ref[...] = jnp.full(m_ref.shape, -jnp.inf, F32)
    acc_ref[...] = jnp.zeros(acc_ref.shape, F32)
    scores(s0_ref, 0)

    def pair(jj, c):
        scores(s1_ref, 2 * jj + 1)
        accumulate(s0_ref, 2 * jj, False)
        scores(s0_ref, 2 * jj + 2)
        accumulate(s1_ref, 2 * jj + 1, False)
        return c

    lax.fori_loop(0, i // 2, pair, 0)

    @pl.when(i % 2 == 0)
    def _():
        accumulate(s0_ref, i, True)

    @pl.when(i % 2 == 1)
    def _():
        scores(s1_ref, i)
        accumulate(s0_ref, i - 1, False)
        accumulate(s1_ref, i, True)

    for g in heads:
        acc = acc_ref[g]
        o_ref[:, g * MLA_V:(g + 1) * MLA_V] = (acc[:, :MLA_V] / acc[:, MLA_V:]).astype(o_ref.dtype)


def _attn_prompt(q, k, v, batch, seq):
    t = ATTN_TILE
    nq = seq // t
    g = HEADS_PER_STEP
    return pl.pallas_call(
        functools.partial(_attn_prompt_kernel, tile=t),
        grid=(batch, MLA_HEADS // g, nq),
        in_specs=[pl.BlockSpec((t, g * HEAD_K), lambda b, h, i: (b * nq + i, h)),
                  pl.BlockSpec((seq, g * HEAD_K), lambda b, h, i: (b, h)),
                  pl.BlockSpec((seq, g * HEAD_V), lambda b, h, i: (b, h))],
        out_specs=pl.BlockSpec((t, g * MLA_V), lambda b, h, i: (b * nq + i, h)),
        out_shape=jax.ShapeDtypeStruct((batch * seq, MLA_HEADS * MLA_V), BF16),
        scratch_shapes=[pltpu.VMEM((g, t, t), F32), pltpu.VMEM((g, t, t), F32),
                        pltpu.VMEM((g, t, 1), F32), pltpu.VMEM((g, t, HEAD_V), F32)],
        compiler_params=_cparams("arbitrary", "arbitrary", "arbitrary"),
        name="attn_prompt",
    )(q, k, v)


def _attn_sample_kernel(q_ref, cp_ref, pp_ref, cn_ref, pn_ref, wk_ref, wv_ref, o_ref, *, n_new):
    rows = q_ref.shape[0]
    q_abs, q_pe = [], []
    for hd in range(MLA_HEADS):
        a = hd * HEAD_K
        q_abs.append(lax.dot_general(q_ref[:, a:a + MLA_NOPE],
                                     wk_ref[:, hd * MLA_NOPE:(hd + 1) * MLA_NOPE], _NT,
                                     preferred_element_type=F32).astype(BF16))
        q_pe.append(q_ref[:, a + MLA_NOPE:a + MLA_NOPE + MLA_ROPE])
    q_abs = jnp.concatenate(q_abs, axis=0)
    q_pe = jnp.concatenate(q_pe, axis=0)
    c_old, c_new = cp_ref[...].astype(BF16), cn_ref[...].astype(BF16)

    def score(c, kpe_ref):
        return (lax.dot_general(q_abs, c, _NT, preferred_element_type=F32)
                + lax.dot_general(q_pe, kpe_ref[...].astype(BF16), _NT, preferred_element_type=F32))

    s1 = score(c_old, pp_ref)
    s2 = score(c_new, pn_ref)
    s2 = jnp.where(lax.broadcasted_iota(jnp.int32, s2.shape, 1) < n_new, s2, -jnp.inf)
    m = jnp.maximum(jnp.max(s1, axis=-1, keepdims=True), jnp.max(s2, axis=-1, keepdims=True))
    p1 = jnp.exp2(s1 - m).astype(BF16)
    p2 = jnp.exp2(s2 - m).astype(BF16)
    l = (jnp.sum(p1.astype(F32), axis=-1, keepdims=True)
         + jnp.sum(p2.astype(F32), axis=-1, keepdims=True))
    o_lat = (jnp.dot(p1, c_old, preferred_element_type=F32)
             + jnp.dot(p2, c_new, preferred_element_type=F32)) / l
    o_lat = o_lat.astype(BF16)
    for hd in range(MLA_HEADS):
        o_ref[:, hd * MLA_V:(hd + 1) * MLA_V] = jnp.dot(
            o_lat[hd * rows:(hd + 1) * rows], wv_ref[:, hd * MLA_V:(hd + 1) * MLA_V],
            preferred_element_type=F32).astype(o_ref.dtype)


def _attn_sample(q, ckv_past, kpe_past, ckv, kpe, wk, wv, dec_batch, past, n_new, first_chunk):
    c = SGU_CHUNK
    new = lambda w: pl.BlockSpec((c, w), lambda b: (first_chunk + b, 0))
    old = lambda w: pl.BlockSpec((past, w), lambda b: (b, 0))
    kv_lora = ckv.shape[1]
    return pl.pallas_call(
        functools.partial(_attn_sample_kernel, n_new=n_new),
        grid=(dec_batch,),
        in_specs=[new(MLA_HEADS * HEAD_K), old(kv_lora), old(MLA_ROPE), new(kv_lora), new(MLA_ROPE),
                  _const_spec(wk.shape), _const_spec(wv.shape)],
        out_specs=pl.BlockSpec((c, MLA_HEADS * MLA_V), lambda b: (b, 0)),
        out_shape=jax.ShapeDtypeStruct((dec_batch * c, MLA_HEADS * MLA_V), BF16),
        compiler_params=_cparams("arbitrary"),
        name="attn_sample",
    )(q, ckv_past, kpe_past, ckv, kpe, wk, wv)


def _route(h, wr_ref, br_ref, n_exp):
    hi = h.astype(BF16)
    lo = (h - hi.astype(F32)).astype(BF16)
    a = jnp.dot(hi, wr_ref[...], preferred_element_type=F32)
    b = jnp.dot(lo, wr_ref[:, :LANE], preferred_element_type=F32)
    logits = a[:, :LANE] + (a[:, LANE:] + b) + br_ref[...]
    lane = lax.broadcasted_iota(jnp.int32, logits.shape, 1)
    lg = jnp.where(lane < n_exp, logits, -jnp.inf)
    v1 = jnp.max(lg, axis=-1, keepdims=True)
    i1 = jnp.min(jnp.where(lg == v1, lane, LANE), axis=-1, keepdims=True)
    lg2 = jnp.where(lane == i1, -jnp.inf, lg)
    v2 = jnp.max(lg2, axis=-1, keepdims=True)
    i2 = jnp.min(jnp.where(lg2 == v2, lane, LANE), axis=-1, keepdims=True)
    e = jnp.exp(v2 - v1)
    g1 = 1.0 / (1.0 + e)
    g2 = e / (1.0 + e)
    out = jnp.where(lane == 0, i1.astype(F32), 0.0)
    out = jnp.where(lane == 1, i2.astype(F32), out)
    out = jnp.where(lane == 2, g1, out)
    return jnp.where(lane == 3, g2, out)


def _pack_bf16_pairs(h):
    w = h.shape[1] // 2
    hi = pltpu.bitcast(h[:, :w].astype(BF16).astype(F32), U32)
    lo = pltpu.bitcast(h[:, w:].astype(BF16).astype(F32), U32)
    return (hi & jnp.uint32(0xFFFF0000)) | (lo >> 16)


def _unpack_bf16_pairs(w):
    hi = pltpu.bitcast(w & jnp.uint32(0xFFFF0000), F32).astype(BF16)
    lo = pltpu.bitcast(w << 16, F32).astype(BF16)
    return hi, lo


def _out_kernel(*refs, attn_starts, x_starts, alpha, n_exp, routed):
    na, nx = len(attn_starts), len(x_starts)
    attn_refs, x_refs, refs = refs[:na], refs[na:na + nx], refs[na + nx:]
    if routed:
        (z_ref, zprev_ref, bg_ref, u_ref, v_ref, cw_ref, sw_ref, sb_ref,
         g1_ref, sh2_ref, sc2_ref, lng_ref, lnb_ref, wo_ref, wr_ref, br_ref,
         x_out, h_out, route_out, mix_ref) = refs
    else:
        (z_ref, zprev_ref, bg_ref, u_ref, v_ref, cw_ref, sw_ref, sb_ref,
         g1_ref, sh2_ref, sc2_ref, lng_ref, lnb_ref, wo_ref,
         x_out, h_out, mix_ref) = refs
    t, d = x_refs[0].shape
    nc = t // SGU_CHUNK
    n_attn = attn_refs[0].shape[1]
    conv_dim = z_ref.shape[1]

    z = z_ref[...]
    zp = zprev_ref[...]
    pos = lax.broadcasted_iota(jnp.int32, (nc, SGU_CHUNK, conv_dim), 1)
    r1 = pltpu.roll(z, 1, 0).reshape(nc, SGU_CHUNK, conv_dim)
    r2 = pltpu.roll(z, 2, 0).reshape(nc, SGU_CHUNK, conv_dim)
    r1 = jnp.where(pos == 0, zp[:, 1:2, :], r1)
    r2 = jnp.where(pos == 0, zp[:, 0:1, :], jnp.where(pos == 1, zp[:, 1:2, :], r2))
    cw = cw_ref[...]
    y = cw[0:1, :] * r2 + cw[1:2, :] * r1 + cw[2:3, :] * z.reshape(nc, SGU_CHUNK, conv_dim)
    conv = bg_ref[...] * y.reshape(t, conv_dim)
    mix_ref[:, :n_attn] = _read_parts(attn_refs, attn_starts)
    mix_ref[:, n_attn:n_attn + conv_dim] = conv.astype(BF16)

    tri = (lax.broadcasted_iota(jnp.int32, (SGU_CHUNK, SGU_CHUNK), 0)
           >= lax.broadcasted_iota(jnp.int32, (SGU_CHUNK, SGU_CHUNK), 1))
    hw = v_ref.shape[1] // SGU_HEADS
    for hd in range(SGU_HEADS):
        w_m = jnp.where(tri, sw_ref[hd], 0.0).astype(BF16)
        bias = sb_ref[:, hd:hd + 1]
        for c in range(nc):
            rows = slice(c * SGU_CHUNK, (c + 1) * SGU_CHUNK)
            cols = slice(hd * hw, (hd + 1) * hw)
            mixed = jnp.dot(w_m, v_ref[rows, cols].astype(BF16), preferred_element_type=F32) + bias
            mix_ref[rows, n_attn + conv_dim + hd * hw:n_attn + conv_dim + (hd + 1) * hw] = (
                u_ref[rows, cols] * mixed).astype(BF16)

    o = jnp.dot(mix_ref[...], wo_ref[...], preferred_element_type=F32)
    x3 = _read_parts(x_refs, x_starts).reshape(nc, SGU_CHUNK, d)
    yres = (alpha * x3 + (1.0 + g1_ref[...]) * o.reshape(nc, SGU_CHUNK, d)).reshape(t, d)
    x1 = _ln(yres, lng_ref[...], lnb_ref[...])
    x_out[...] = x1
    h2 = (x1.reshape(nc, SGU_CHUNK, d) * (1.0 + sc2_ref[...]) + sh2_ref[...]).reshape(t, d)
    if routed:
        _store_token_tiles(h_out, _pack_bf16_pairs(h2))
        route_out[...] = _route(h2, wr_ref, br_ref, n_exp)
    else:
        h_out[...] = h2.astype(BF16)


def _mixer_out(attn_parts, z, zprev, bg, u, v, conv_w, sgu_w, sgu_bt, x_parts, mod3, lng, lnb,
               w_o, alpha, router=None):
    n, d = z.shape[0], x_parts[0].shape[1]
    t = ROW_TILE
    nc = t // SGU_CHUNK
    routed = router is not None
    row = lambda w: pl.BlockSpec((t, w), lambda i: (i, 0))
    mod = lambda k: pl.BlockSpec((nc, 1, d), lambda i: (i, 0, k))
    attn_specs, attn_starts = _part_specs(attn_parts, t)
    x_specs, x_starts = _part_specs(x_parts, t)
    in_specs = attn_specs + x_specs + [
        row(z.shape[1]), pl.BlockSpec((nc,) + zprev.shape[1:], lambda i: (i, 0, 0)),
        row(bg.shape[1]), row(u.shape[1]), row(v.shape[1]),
        _const_spec(conv_w.shape), _const_spec(sgu_w.shape), _const_spec(sgu_bt.shape),
        mod(2), mod(3), mod(4), _const_spec(lng.shape), _const_spec(lnb.shape),
        _const_spec(w_o.shape)]
    args = [*attn_parts, *x_parts, z, zprev, bg, u, v, conv_w, sgu_w, sgu_bt, mod3, mod3, mod3,
            lng, lnb, w_o]
    out_specs = [row(d)]
    out_shape = [jax.ShapeDtypeStruct((n, d), F32)]
    n_exp = 0
    if routed:
        w_r, b_r, n_exp = router
        in_specs += [_const_spec(w_r.shape), _const_spec(b_r.shape)]
        args += [w_r, b_r]
        hw = d // 2 // LANE
        out_specs += [pl.BlockSpec((t * hw, LANE), lambda i: (i, 0)), row(LANE)]
        out_shape += [jax.ShapeDtypeStruct((n * hw, LANE), U32), jax.ShapeDtypeStruct((n, LANE), F32)]
    else:
        out_specs += [row(d)]
        out_shape += [jax.ShapeDtypeStruct((n, d), BF16)]
    return pl.pallas_call(
        functools.partial(_out_kernel, attn_starts=attn_starts, x_starts=x_starts, alpha=alpha,
                          n_exp=n_exp, routed=routed),
        grid=(n // t,),
        in_specs=in_specs,
        out_specs=out_specs,
        out_shape=out_shape,
        scratch_shapes=[pltpu.VMEM((t, d), BF16)],
        compiler_params=_cparams("arbitrary"),
        name="mixer_out_routed" if routed else "mixer_out",
    )(*args)


def _ffn_kernel(h_ref, w1_ref, w3_ref, w2_ref, x_ref, g2_ref, lng_ref, lnb_ref, o_ref, acc_ref,
                *, alpha):
    j = pl.program_id(1)

    @pl.when(j == 0)
    def _():
        acc_ref[...] = jnp.zeros_like(acc_ref)

    h = h_ref[...]
    a = jnp.dot(h, w1_ref[...], preferred_element_type=F32)
    b = jnp.dot(h, w3_ref[...], preferred_element_type=F32)
    acc_ref[...] += jnp.dot((_silu(a) * b).astype(BF16), w2_ref[...], preferred_element_type=F32)

    @pl.when(j == pl.num_programs(1) - 1)
    def _():
        t, d = x_ref.shape
        nc = t // SGU_CHUNK
        x3 = x_ref[...].reshape(nc, SGU_CHUNK, d)
        y = alpha * x3 + (1.0 + g2_ref[...]) * acc_ref[...].reshape(nc, SGU_CHUNK, d)
        o_ref[...] = _ln(y.reshape(t, d), lng_ref[...], lnb_ref[...])


def _ffn_dense(h, w1, w3, w2, x, mod3, lng, lnb, alpha):
    n, d = x.shape
    f = w1.shape[1]
    t, tf = ROW_TILE, FF_TILE
    nc = t // SGU_CHUNK
    return pl.pallas_call(
        functools.partial(_ffn_kernel, alpha=alpha),
        grid=(n // t, f // tf),
        in_specs=[pl.BlockSpec((t, d), lambda i, j: (i, 0)),
                  pl.BlockSpec((d, tf), lambda i, j: (0, j)),
                  pl.BlockSpec((d, tf), lambda i, j: (0, j)),
                  pl.BlockSpec((tf, d), lambda i, j: (j, 0)),
                  pl.BlockSpec((t, d), lambda i, j: (i, 0)),
                  pl.BlockSpec((nc, 1, d), lambda i, j: (i, 0, 5)),
                  _const_spec(lng.shape), _const_spec(lnb.shape)],
        out_specs=pl.BlockSpec((t, d), lambda i, j: (i, 0)),
        out_shape=jax.ShapeDtypeStruct((n, d), F32),
        scratch_shapes=[pltpu.VMEM((t, d), F32)],
        compiler_params=_cparams("arbitrary", "arbitrary"),
        name="ffn_dense",
    )(h, w1, w3, w2, x, mod3, lng, lnb)


def _moe_kernel(te_ref, tv_ref, rt_ref, h_hbm, w1_ref, w3_ref, w2_ref, o_ref,
                hbuf, hb, acc_ref, sem, *, tm, n_tiles):
    r = pl.program_id(0)
    j = pl.program_id(1)
    half = hb.shape[1] // 2
    wt = half // LANE

    def row_copy(tile, slot, i):
        tok = rt_ref[tile * tm + i]
        return pltpu.make_async_copy(h_hbm.at[pl.ds(pl.multiple_of(tok * wt, wt), wt), :],
                                     hbuf.at[slot, pl.ds(pl.multiple_of(i * wt, wt), wt), :],
                                     sem.at[slot])

    def start_tile(tile, slot):
        def body(i, c):
            row_copy(tile, slot, i).start()
            return c
        lax.fori_loop(0, tm, body, 0, unroll=GATHER_UNROLL)

    def wait_tile(tile, slot):
        def body(i, c):
            row_copy(tile, slot, i).wait()
            return c
        lax.fori_loop(0, tm, body, 0, unroll=GATHER_UNROLL)

    slot = r % 2

    @pl.when(j == 0)
    def _():
        @pl.when(r == 0)
        def _():
            start_tile(0, 0)

        wait_tile(r, slot)

        @pl.when(r + 1 < n_tiles)
        def _():
            start_tile(r + 1, 1 - slot)

        for c in range(wt):
            hi, lo = _unpack_bf16_pairs(hbuf[slot, pl.ds(c, tm, stride=wt), :])
            hb[:, c * LANE:(c + 1) * LANE] = hi
            hb[:, half + c * LANE:half + (c + 1) * LANE] = lo
        acc_ref[...] = jnp.zeros_like(acc_ref)

    @pl.when(tv_ref[r] == 1)
    def _():
        h = hb[...]
        a = jnp.dot(h, w1_ref[...], preferred_element_type=F32)
        b = jnp.dot(h, w3_ref[...], preferred_element_type=F32)
        acc_ref[...] += jnp.dot((_silu(a) * b).astype(BF16), w2_ref[...],
                                preferred_element_type=F32)

    @pl.when(j == pl.num_programs(1) - 1)
    def _():
        _store_token_tiles(o_ref, acc_ref[...])


def _store_token_tiles(o_ref, x):
    t, w = x.shape
    wt = w // LANE
    for c in range(wt):
        o_ref[pl.ds(c, t, stride=wt), :] = x[:, c * LANE:(c + 1) * LANE]


def _moe_experts(h_packed, w1, w3, w2, tile_expert, tile_valid, row_token):
    n_exp, d, f = w1.shape
    p = row_token.shape[0]
    tm, tf = ROW_TILE, MOE_FF_TILE
    assert f % tf == 0
    n_tiles = p // tm
    nj = f // tf
    hw, ow = d // 2 // LANE, d // LANE
    col = lambda j, tv, r: j * tv[r] + (nj - 1) * (1 - tv[r])
    grid_spec = pltpu.PrefetchScalarGridSpec(
        num_scalar_prefetch=3,
        grid=(n_tiles, nj),
        in_specs=[pl.BlockSpec(memory_space=pl.ANY),
                  pl.BlockSpec((None, d, tf), lambda r, j, te, tv, rt: (te[r], 0, col(j, tv, r))),
                  pl.BlockSpec((None, d, tf), lambda r, j, te, tv, rt: (te[r], 0, col(j, tv, r))),
                  pl.BlockSpec((None, tf, d), lambda r, j, te, tv, rt: (te[r], col(j, tv, r), 0))],
        out_specs=pl.BlockSpec((tm * ow, LANE), lambda r, j, te, tv, rt: (r, 0)),
        scratch_shapes=[pltpu.VMEM((2, tm * hw, LANE), U32), pltpu.VMEM((tm, d), BF16),
                        pltpu.VMEM((tm, d), F32), pltpu.SemaphoreType.DMA((2,))],
    )
    return pl.pallas_call(
        functools.partial(_moe_kernel, tm=tm, n_tiles=n_tiles),
        grid_spec=grid_spec,
        out_shape=jax.ShapeDtypeStruct((p * ow, LANE), F32),
        compiler_params=_cparams("arbitrary", "arbitrary"),
        name="moe_experts",
    )(tile_expert, tile_valid, row_token, h_packed, w1, w3, w2)


def _combine_kernel(pos_ref, ys_hbm, route_ref, x_ref, g2_ref, lng_ref, lnb_ref, o_ref,
                    ybuf, fbuf, sem, *, tc, n_tiles, alpha):
    i = pl.program_id(0)
    t, d = x_ref.shape
    wt = d // LANE

    def row_copy(tile, slot, k, r):
        p = pos_ref[(tile * tc + r) * TOP_K + k]
        return pltpu.make_async_copy(ys_hbm.at[pl.ds(pl.multiple_of(p * wt, wt), wt), :],
                                     ybuf.at[slot, k, pl.ds(pl.multiple_of(r * wt, wt), wt), :],
                                     sem.at[slot])

    def start_tile(tile, slot):
        def body(r, c):
            for k in range(TOP_K):
                row_copy(tile, slot, k, r).start()
            return c
        lax.fori_loop(0, tc, body, 0, unroll=GATHER_UNROLL)

    def wait_tile(tile, slot):
        def body(r, c):
            for k in range(TOP_K):
                row_copy(tile, slot, k, r).wait()
            return c
        lax.fori_loop(0, tc, body, 0, unroll=GATHER_UNROLL)

    slot = i % 2

    @pl.when(i == 0)
    def _():
        start_tile(0, 0)

    wait_tile(i, slot)

    @pl.when(i + 1 < n_tiles)
    def _():
        start_tile(i + 1, 1 - slot)

    nc = t // SGU_CHUNK
    gates = [route_ref[:, TOP_K + k:TOP_K + k + 1] for k in range(TOP_K)]
    for c in range(wt):
        f = gates[0] * ybuf[slot, 0, pl.ds(c, t, stride=wt), :]
        for k in range(1, TOP_K):
            f = f + gates[k] * ybuf[slot, k, pl.ds(c, t, stride=wt), :]
        fbuf[:, c * LANE:(c + 1) * LANE] = f
    x3 = x_ref[...].reshape(nc, SGU_CHUNK, d)
    y = alpha * x3 + (1.0 + g2_ref[...]) * fbuf[...].reshape(nc, SGU_CHUNK, d)
    o_ref[...] = _ln(y.reshape(t, d), lng_ref[...], lnb_ref[...])


def _moe_combine(ys, pos, route, x, mod3, lng, lnb, alpha):
    n, d = x.shape
    tc = COMBINE_TILE
    nc = tc // SGU_CHUNK
    n_tiles = n // tc
    wt = d // LANE
    grid_spec = pltpu.PrefetchScalarGridSpec(
        num_scalar_prefetch=1,
        grid=(n_tiles,),
        in_specs=[pl.BlockSpec(memory_space=pl.ANY),
                  pl.BlockSpec((tc, LANE), lambda i, pos: (i, 0)),
                  pl.BlockSpec((tc, d), lambda i, pos: (i, 0)),
                  pl.BlockSpec((nc, 1, d), lambda i, pos: (i, 0, 5)),
                  _const_spec(lng.shape), _const_spec(lnb.shape)],
        out_specs=pl.BlockSpec((tc, d), lambda i, pos: (i, 0)),
        scratch_shapes=[pltpu.VMEM((2, TOP_K, tc * wt, LANE), F32), pltpu.VMEM((tc, d), F32),
                        pltpu.SemaphoreType.DMA((2,))],
    )
    return pl.pallas_call(
        functools.partial(_combine_kernel, tc=tc, n_tiles=n_tiles, alpha=alpha),
        grid_spec=grid_spec,
        out_shape=jax.ShapeDtypeStruct((n, d), F32),
        compiler_params=_cparams("arbitrary"),
        name="moe_combine",
    )(pos.reshape(-1), ys, route, x, mod3, lng, lnb)


def _dispatch_tables(route, real, n_exp, tm):
    n = real.shape[0]
    real2 = jnp.asarray(np.repeat(real, TOP_K))
    e = route[:, :TOP_K].astype(jnp.int32).reshape(-1)
    onehot = ((e[:, None] == jnp.arange(n_exp, dtype=jnp.int32)[None, :]) & real2[:, None]).astype(jnp.int32)
    csum = jnp.cumsum(onehot, axis=0)
    rank = jnp.take_along_axis(csum, e[:, None], axis=1)[:, 0] - 1
    counts = csum[-1]
    padded = ((counts + tm - 1) // tm) * tm
    ends = jnp.cumsum(padded)
    dest = (ends - padded)[e] + rank
    p = -(-(TOP_K * int(real.sum())) // tm) * tm + n_exp * tm
    tok = jnp.repeat(jnp.arange(n, dtype=jnp.int32), TOP_K)
    row_token = jnp.zeros((p,), jnp.int32).at[jnp.where(real2, dest, p)].set(tok, mode="drop")
    tile_start = jnp.arange(p // tm, dtype=jnp.int32) * tm
    tile_expert = jnp.minimum(jnp.sum(tile_start[:, None] >= ends[None, :], axis=1), n_exp - 1)
    tile_valid = (tile_start < ends[-1]).astype(jnp.int32)
    pos = jnp.where(real2, dest, 0)
    return tile_expert.astype(jnp.int32), tile_valid, row_token, pos


def _rot_cols(w):
    half = w.shape[-1] // 2
    return jnp.concatenate([-w[..., half:], w[..., :half]], axis=-1)


def _pad_cols(w, width):
    return jnp.pad(w, ((0, 0), (0, width - w.shape[1])))


def _layout_w_in(w, q_lora, kv_lora):
    o = q_lora + kv_lora
    kpe = w[:, o:o + MLA_ROPE]
    return jnp.concatenate([w[:, :o], _pad_cols(kpe, LANE), _pad_cols(_rot_cols(kpe), LANE),
                            w[:, o + MLA_ROPE:]], axis=1).astype(BF16)


def _layout_w_uq(w):
    per = MLA_NOPE + MLA_ROPE
    main, rot = [], []
    for hd in range(MLA_HEADS):
        pe = w[:, hd * per + MLA_NOPE:(hd + 1) * per]
        main += [w[:, hd * per:hd * per + MLA_NOPE], _pad_cols(pe, LANE)]
        rot += [_pad_cols(_rot_cols(pe), LANE)]
    return jnp.concatenate(main, axis=1).astype(BF16), jnp.concatenate(rot, axis=1).astype(BF16)


def _layout_w_ukv(w):
    per = MLA_NOPE + MLA_V
    wk = jnp.concatenate([w[:, hd * per:hd * per + MLA_NOPE] for hd in range(MLA_HEADS)], axis=1)
    wv = jnp.concatenate([w[:, hd * per + MLA_NOPE:(hd + 1) * per] for hd in range(MLA_HEADS)], axis=1)
    return wk.astype(BF16), wv.astype(BF16)


def _rope_table(pos):
    half = MLA_ROPE // 2
    inv = ROPE_THETA ** (-jnp.arange(half, dtype=F32) / half)
    ang = pos.astype(F32)[:, None] * inv[None, :]
    pad = jnp.zeros((pos.shape[0], LANE - MLA_ROPE), F32)
    cos, sin = jnp.cos(ang), jnp.sin(ang)
    return jnp.concatenate([cos, cos, pad, sin, sin, pad], axis=1)


def kernel(x_prompt, x_sample, cache_ckv, cache_kpe, state_conv, c_prompt, c_sample, w_ada, b_ada, w_in, q_norm_g, kv_norm_g, w_uq, w_ukv, conv_w, sgu_ln_g, sgu_ln_b, sgu_w, sgu_b, w_o, ln_g, ln_b, ffn_w1, ffn_w3, ffn_w2, router_w, router_b, exp_w1, exp_w3, exp_w2):
    batch, seq, d = x_prompt.shape
    dec_batch, dec_seq, _ = x_sample.shape
    depth, _, past, kv_lora = cache_ckv.shape
    q_lora = q_norm_g.shape[1]
    conv_dim = conv_w.shape[2]
    sgu_dim = sgu_ln_g.shape[1]
    n_exp = router_w.shape[2]
    c = SGU_CHUNK
    assert seq % ATTN_TILE == 0 and (batch * seq) % ROW_TILE == 0
    assert CONV_WIDTH - 1 <= dec_seq <= c and (dec_batch * c) % ROW_TILE == 0
    assert sgu_dim == SGU_HEADS * LANE and (dec_batch * past) % ROW_TILE == 0
    alpha = (2 * depth) ** 0.25

    n_p = batch * seq
    n_s = dec_batch * c
    n = n_p + n_s
    n_chunks = n // c
    chunk_batch = np.concatenate([np.arange(n_p // c) // (seq // c), batch + np.arange(dec_batch)])
    real = np.concatenate([np.ones(n_p, bool), np.tile(np.arange(c) < dec_seq, dec_batch)])
    pos = jnp.concatenate([jnp.tile(jnp.arange(seq), batch),
                           jnp.tile(past + jnp.arange(c), dec_batch)])
    cs = _rope_table(pos)

    x_parts = [x_prompt.reshape(n_p, d),
               jnp.pad(x_sample, ((0, 0), (0, c - dec_seq), (0, 0))).reshape(n_s, d)]
    n_c = batch + dec_batch
    c_all = jnp.pad(jnp.concatenate([c_prompt, c_sample], axis=0), ((0, -n_c % 8), (0, 0)))
    mod = _adaln_mod(c_all, w_ada, b_ada)

    outs = {k: [] for k in ("ckv_p", "kpe_p", "conv_p", "ckv_s", "kpe_s", "conv_s", "v_s")}
    keep = min(seq, past)
    for l in range(depth):
        mod3 = mod[l][chunk_batch].reshape(n_chunks, 1, 6 * d)
        w_in_p = _layout_w_in(w_in[l], q_lora, kv_lora)
        wq_p, wq_rot = _layout_w_uq(w_uq[l])
        wk, wv = _layout_w_ukv(w_ukv[l])
        row2 = lambda a: a.reshape(1, -1)

        q, ckv, kpe, z, bg, u, v = _in_proj(
            x_parts, mod3, cs, w_in_p, row2(q_norm_g[l]), row2(kv_norm_g[l]), wq_p, wq_rot,
            row2(sgu_ln_g[l]), row2(sgu_ln_b[l]))

        k_new, v_new = _kv_expand(ckv, kpe, wk, wv)
        attn_parts = [
            _attn_prompt(q, k_new, v_new, batch, seq),
            _attn_sample(q, cache_ckv[l].reshape(dec_batch * past, kv_lora),
                         cache_kpe[l].reshape(dec_batch * past, MLA_ROPE), ckv, kpe, wk, wv,
                         dec_batch, past, dec_seq, n_p // c)]

        z_p = z[:n_p].reshape(batch, seq // c, c, conv_dim)[:, :, c - (CONV_WIDTH - 1):, :]
        zprev = jnp.concatenate([
            jnp.concatenate([jnp.zeros_like(z_p[:, :1]), z_p[:, :-1]], axis=1).reshape(n_p // c, CONV_WIDTH - 1, conv_dim),
            state_conv[l]], axis=0)

        moe = l % 2 == 1
        i = l // 2
        router = None
        if moe:
            w_r = _pad_cols(router_w[i], LANE)
            w_r_hi = w_r.astype(BF16)
            w_r_lo = (w_r - w_r_hi.astype(F32)).astype(BF16)
            router = (jnp.concatenate([w_r_hi, w_r_lo], axis=1),
                      _pad_cols(router_b[i].reshape(1, n_exp), LANE), n_exp)
        res = _mixer_out(attn_parts, z, zprev, bg, u, v, conv_w[l], sgu_w[l].astype(F32), sgu_b[l].T,
                         x_parts, mod3, row2(ln_g[l, 0]), row2(ln_b[l, 0]), w_o[l].astype(BF16),
                         alpha, router=router)
        if moe:
            x1, h2, route = res
            te, tv, row_token, rpos = _dispatch_tables(route, real, n_exp, ROW_TILE)
            ys = _moe_experts(h2, exp_w1[i].astype(BF16), exp_w3[i].astype(BF16),
                              exp_w2[i].astype(BF16), te, tv, row_token)
            x = _moe_combine(ys, rpos, route, x1, mod3, row2(ln_g[l, 1]), row2(ln_b[l, 1]), alpha)
        else:
            x1, h2 = res
            x = _ffn_dense(h2, ffn_w1[i].astype(BF16), ffn_w3[i].astype(BF16),
                           ffn_w2[i].astype(BF16), x1, mod3, row2(ln_g[l, 1]), row2(ln_b[l, 1]), alpha)
        x_parts = [x]

        sample = lambda a: a[n_p:].reshape(dec_batch, c, a.shape[1])[:, :dec_seq]
        outs["ckv_p"].append(ckv[:n_p].reshape(batch, seq, kv_lora)[:, seq - keep:])
        outs["kpe_p"].append(kpe[:n_p].reshape(batch, seq, MLA_ROPE)[:, seq - keep:])
        outs["conv_p"].append(z[:n_p].reshape(batch, seq, conv_dim)[:, seq - (CONV_WIDTH - 1):])
        outs["ckv_s"].append(sample(ckv))
        outs["kpe_s"].append(sample(kpe))
        outs["conv_s"].append(sample(z)[:, dec_seq - (CONV_WIDTH - 1):])
        outs["v_s"].append(sample(v))

    y_p = x[:n_p].reshape(batch, seq, d)
    y_s = x[n_p:].reshape(dec_batch, c, d)[:, :dec_seq]
    st = lambda k: jnp.stack(outs[k])
    return (y_p, y_s, st("ckv_p"), st("kpe_p"), st("conv_p"), st("ckv_s"), st("kpe_s"),
            st("conv_s"), st("v_s"))
```
